```python
import math
import numpy as np
import jax
import jax.numpy as jnp
from jax import lax


D_MODEL = 1024
BATCH = 8
SEQ = 8192
DEPTH = 4
DEC_BATCH = 4
DEC_SEQ = 4096
PAST_LEN = 128

N_MEM = 256
EPS = 1e-6
D_FF = 2816
SGU_WIDTH = 512
SGU_GROUPS = 4
SGU_CHUNK = 128
SGU_GDIM = SGU_WIDTH // SGU_GROUPS
POOL_WIDTH = 512
POOL_WINDOWS = (2, 4, 8, 16)
POOL_GROUPS = len(POOL_WINDOWS)
POOL_GDIM = POOL_WIDTH // POOL_GROUPS
SSD_INNER = 1024
SSD_HEADDIM = 64
SSD_HEADS = SSD_INNER // SSD_HEADDIM
SSD_GROUPS = 2
SSD_HPG = SSD_HEADS // SSD_GROUPS
SSD_STATE = 64
SSD_CHUNK = 128
SSD_CONV = 4
SSD_CONV_LEFT = 2
SSD_GN = SSD_GROUPS * SSD_STATE
SSD_CONV_DIM = SSD_INNER + 2 * SSD_GN
DT_MIN = 0.001
DT_MAX = 0.1
XA_HEADS = 4
XA_HEADDIM = D_MODEL // XA_HEADS
N_BRANCH = 3
IN_SPLITS = (SGU_WIDTH, 2 * SGU_WIDTH, 2 * SGU_WIDTH + POOL_WIDTH, 2 * SGU_WIDTH + POOL_WIDTH + SSD_INNER, 2 * SGU_WIDTH + POOL_WIDTH + SSD_INNER + SSD_CONV_DIM)
N_IN = 2 * SGU_WIDTH + POOL_WIDTH + SSD_INNER + SSD_CONV_DIM + 2 * SSD_HEADS

kernel_name = 'hybrid_gated_sgu_pool_ssd_encoder'


def rms_norm(x, g):
    xf = x.astype(jnp.float32)
    y = xf * lax.rsqrt(jnp.mean(xf * xf, axis=-1, keepdims=True) + EPS)
    return (y * g.astype(jnp.float32)).astype(x.dtype)


def layer_norm(x, g, b):
    xf = x.astype(jnp.float32)
    xc = xf - jnp.mean(xf, axis=-1, keepdims=True)
    var = jnp.mean(xc * xc, axis=-1, keepdims=True)
    y = xc * lax.rsqrt(var + EPS) * g.astype(jnp.float32) + b.astype(jnp.float32)
    return y.astype(x.dtype)


def swiglu_ffn(x, w13, w2):
    a, b = jnp.split(x @ w13, 2, axis=-1)
    return (jax.nn.silu(a) * b) @ w2


def spatial_gating(u, v, ln_g, ln_b, ws, bias):
    b, s, _ = v.shape
    nc = s // SGU_CHUNK
    vn = layer_norm(v, ln_g, ln_b).reshape(b, nc, SGU_CHUNK, SGU_GROUPS, SGU_GDIM)
    mixed = jnp.einsum('gts,bcsgd->bctgd', ws, vn) + bias.T[None, None, :, :, None]
    return u * mixed.reshape(b, s, SGU_WIDTH)


def multiscale_pool(p, w_grp, scale):
    b, s, _ = p.shape
    pf = p.astype(jnp.float32).reshape(b, s, POOL_GROUPS, POOL_GDIM)
    csum = jnp.concatenate([jnp.zeros_like(pf[:, :1]), jnp.cumsum(pf, axis=1)], axis=1)
    pos = np.arange(s)
    outs = []
    for gi, w in enumerate(POOL_WINDOWS):
        lo = np.clip(pos - w // 2, 0, s)
        hi = np.clip(pos + w - w // 2, 0, s)
        cnt = jnp.asarray((hi - lo).astype(np.float32))[None, :, None]
        mean = (csum[:, hi, gi] - csum[:, lo, gi]) / cnt
        outs.append(mean - pf[:, :, gi])
    pooled = jnp.stack(outs, axis=2).astype(p.dtype)
    mixed = jnp.einsum('bsgd,gde->bsge', pooled, w_grp).reshape(b, s, POOL_WIDTH)
    return mixed * scale


def centred_dwconv(x, w, bias):
    s = x.shape[1]
    xp = jnp.pad(x, ((0, 0), (SSD_CONV_LEFT, SSD_CONV - 1 - SSD_CONV_LEFT), (0, 0)))
    y = bias
    for k in range(SSD_CONV):
        y = y + xp[:, k:k + s] * w[k]
    return y


def ssd_scan(x, dt, a, bm, cm):
    b, s = x.shape[:2]
    q = SSD_CHUNK
    nc = s // q
    xs = (x * dt[..., None]).reshape(b, nc, q, SSD_GROUPS, SSD_HPG, SSD_HEADDIM)
    da = (dt * a).reshape(b, nc, q, SSD_GROUPS, SSD_HPG)
    bm = bm.reshape(b, nc, q, SSD_GROUPS, SSD_STATE)
    cm = cm.reshape(b, nc, q, SSD_GROUPS, SSD_STATE)
    acum = jnp.cumsum(da, axis=2)
    diff = acum[:, :, :, None] - acum[:, :, None, :]
    mask = jnp.tril(jnp.ones((q, q), dtype=bool))[:, :, None, None]
    decay = jnp.exp(jnp.where(mask, diff, -jnp.inf))
    cb = jnp.einsum('bcqgn,bcsgn->bcqsg', cm, bm)
    y_diag = jnp.einsum('bcqsge,bcsgep->bcqgep', cb[..., None] * decay, xs)
    decay_states = jnp.exp(acum[:, :, -1:] - acum)
    states = jnp.einsum('bcqgn,bcqgep->bcgepn', bm, xs * decay_states[..., None])
    chunk_decay = jnp.exp(acum[:, :, -1])

    def step(h, inp):
        st, dec = inp
        return h * dec[..., None, None] + st, h

    h0 = jnp.zeros((b, SSD_GROUPS, SSD_HPG, SSD_HEADDIM, SSD_STATE), jnp.float32)
    _, prev = lax.scan(step, h0, (jnp.moveaxis(states, 1, 0), jnp.moveaxis(chunk_decay, 1, 0)))
    prev = jnp.moveaxis(prev, 0, 1)
    y_off = jnp.einsum('bcqgn,bcgepn->bcqgep', cm, prev) * jnp.exp(acum)[..., None]
    return (y_diag + y_off).reshape(b, s, SSD_HEADS, SSD_HEADDIM)


def bidirectional_ssd(z, xbc, dt_raw, conv_w, conv_b, dt_bias, a_log, d_skip, norm_g):
    b, s, _ = z.shape
    xbc = jax.nn.silu(centred_dwconv(xbc, conv_w, conv_b)).astype(jnp.float32)
    xc = xbc[..., :SSD_INNER].reshape(b, s, SSD_HEADS, SSD_HEADDIM)
    bm = xbc[..., SSD_INNER:SSD_INNER + SSD_GN].reshape(b, s, SSD_GROUPS, SSD_STATE)
    cm = xbc[..., SSD_INNER + SSD_GN:].reshape(b, s, SSD_GROUPS, SSD_STATE)
    dt = jax.nn.softplus(dt_raw.astype(jnp.float32).reshape(b, s, 2, SSD_HEADS) + dt_bias.astype(jnp.float32))
    a = -jnp.exp(a_log.astype(jnp.float32))
    y_fwd = ssd_scan(xc, dt[:, :, 0], a[0], bm, cm)
    y_bwd = jnp.flip(ssd_scan(jnp.flip(xc, 1), jnp.flip(dt[:, :, 1], 1), a[1], jnp.flip(bm, 1), jnp.flip(cm, 1)), 1)
    y = y_fwd + y_bwd + xc * d_skip.astype(jnp.float32)[:, None]
    y = y.reshape(b, s, SSD_INNER) * jax.nn.silu(z.astype(jnp.float32))
    gsz = SSD_INNER // SSD_GROUPS
    y = rms_norm(y.reshape(b, s, SSD_GROUPS, gsz), norm_g.reshape(SSD_GROUPS, gsz)).reshape(b, s, SSD_INNER)
    return y.astype(z.dtype)


def memory_cross_attention(h, mem, wq, wkv, wo):
    b, s, _ = h.shape
    m = mem.shape[1]
    q = (h @ wq).reshape(b, s, XA_HEADS, XA_HEADDIM)
    k, v = jnp.split(mem @ wkv, 2, axis=-1)
    k = k.reshape(b, m, XA_HEADS, XA_HEADDIM)
    v = v.reshape(b, m, XA_HEADS, XA_HEADDIM)
    scores = jnp.einsum('bshd,bmhd->bhsm', q.astype(jnp.float32), k.astype(jnp.float32)) * (XA_HEADDIM ** -0.5)
    probs = jax.nn.softmax(scores, axis=-1).astype(v.dtype)
    o = jnp.einsum('bhsm,bmhd->bshd', probs, v).reshape(b, s, D_MODEL)
    return o @ wo


def encoder_trunk(x, mem, p):
    bsz, s, _ = x.shape
    for l in range(DEPTH):
        x = x + 0.5 * swiglu_ffn(rms_norm(x, p['ffn1_norm'][l]), p['ffn1_w13'][l], p['ffn1_w2'][l])
        n = rms_norm(x, p['mix_norm'][l])
        u, v, pin, z, xbc, dt_raw = jnp.split(n @ p['w_in'][l], IN_SPLITS, axis=-1)
        a_out = spatial_gating(jax.nn.gelu(u), jax.nn.gelu(v), p['sgu_ln_g'][l], p['sgu_ln_b'][l], p['sgu_ws'][l], p['sgu_bias'][l])
        b_out = multiscale_pool(pin, p['pool_w'][l], p['pool_scale'][l])
        c_out = bidirectional_ssd(z, xbc, dt_raw, p['conv_w'][l], p['conv_b'][l], p['dt_bias'][l], p['a_log'][l], p['d_skip'][l], p['ssd_norm'][l])
        gates = jax.nn.sigmoid(n @ p['w_gate'][l] + p['b_gate'][l]).reshape(bsz, s, N_BRANCH, D_MODEL)
        merged = (gates[:, :, 0] * (a_out @ p['w_branch_a'][l])
                  + gates[:, :, 1] * (b_out @ p['w_branch_b'][l])
                  + gates[:, :, 2] * (c_out @ p['w_branch_c'][l]))
        x = x + merged @ p['w_out'][l]
        x = x + memory_cross_attention(rms_norm(x, p['xattn_norm'][l]), rms_norm(mem, p['mem_norm'][l]), p['xattn_wq'][l], p['xattn_wkv'][l], p['xattn_wo'][l])
        x = x + 0.5 * swiglu_ffn(rms_norm(x, p['ffn2_norm'][l]), p['ffn2_w13'][l], p['ffn2_w2'][l])
    return rms_norm(x, p['final_norm'])


def setup_inputs(seed: int = 0) -> dict:
    key = jax.random.key(seed)
    ks = iter(jax.random.split(key, 48))
    L = DEPTH
    D = D_MODEL

    def normal(shape, scale):
        return jax.random.normal(next(ks), shape, jnp.float32) * scale

    def gain(shape):
        return 1.0 + 0.02 * jax.random.normal(next(ks), shape, jnp.float32)

    out = {}
    out['x_prompt'] = normal((BATCH, SEQ, D), 1.0)
    out['x_sample'] = normal((DEC_BATCH, DEC_SEQ, D), 1.0)
    out['mem_prompt'] = normal((BATCH, N_MEM, D), 1.0)
    out['mem_sample'] = normal((DEC_BATCH, N_MEM, D), 1.0)
    out['ffn1_norm'] = gain((L, D))
    out['ffn1_w13'] = normal((L, D, 2 * D_FF), D ** -0.5)
    out['ffn1_w2'] = normal((L, D_FF, D), D_FF ** -0.5)
    out['mix_norm'] = gain((L, D))
    out['w_in'] = normal((L, D, N_IN), D ** -0.5)
    out['w_gate'] = normal((L, D, N_BRANCH * D), D ** -0.5)
    out['b_gate'] = normal((L, N_BRANCH * D), 0.02)
    out['sgu_ln_g'] = gain((L, SGU_WIDTH))
    out['sgu_ln_b'] = normal((L, SGU_WIDTH), 0.02)
    out['sgu_ws'] = normal((L, SGU_GROUPS, SGU_CHUNK, SGU_CHUNK), SGU_CHUNK ** -0.5)
    out['sgu_bias'] = gain((L, SGU_GROUPS, SGU_CHUNK))
    out['pool_w'] = normal((L, POOL_GROUPS, POOL_GDIM, POOL_GDIM), POOL_GDIM ** -0.5)
    out['pool_scale'] = gain((L, POOL_WIDTH))
    out['conv_w'] = normal((L, SSD_CONV, SSD_CONV_DIM), SSD_CONV ** -0.5)
    out['conv_b'] = normal((L, SSD_CONV_DIM), 0.02)
    dt0 = jnp.exp(jax.random.uniform(next(ks), (L, 2, SSD_HEADS), jnp.float32) * (math.log(DT_MAX) - math.log(DT_MIN)) + math.log(DT_MIN))
    out['dt_bias'] = dt0 + jnp.log(-jnp.expm1(-dt0))
    out['a_log'] = jnp.log(jax.random.uniform(next(ks), (L, 2, SSD_HEADS), jnp.float32, minval=1.0, maxval=16.0))
    out['d_skip'] = gain((L, SSD_HEADS))
    out['ssd_norm'] = gain((L, SSD_INNER))
    out['w_branch_a'] = normal((L, SGU_WIDTH, D), SGU_WIDTH ** -0.5)
    out['w_branch_b'] = normal((L, POOL_WIDTH, D), POOL_WIDTH ** -0.5)
    out['w_branch_c'] = normal((L, SSD_INNER, D), SSD_INNER ** -0.5)
    out['w_out'] = normal((L, D, D), D ** -0.5)
    out['xattn_norm'] = gain((L, D))
    out['mem_norm'] = gain((L, D))
    out['xattn_wq'] = normal((L, D, D), D ** -0.5)
    out['xattn_wkv'] = normal((L, D, 2 * D), D ** -0.5)
    out['xattn_wo'] = normal((L, D, D), D ** -0.5)
    out['ffn2_norm'] = gain((L, D))
    out['ffn2_w13'] = normal((L, D, 2 * D_FF), D ** -0.5)
    out['ffn2_w2'] = normal((L, D_FF, D), D_FF ** -0.5)
    out['final_norm'] = gain((D,))
    return out


def reference(x_prompt, x_sample, mem_prompt, mem_sample, ffn1_norm, ffn1_w13, ffn1_w2, mix_norm, w_in, w_gate, b_gate, sgu_ln_g, sgu_ln_b, sgu_ws, sgu_bias, pool_w, pool_scale, conv_w, conv_b, dt_bias, a_log, d_skip, ssd_norm, w_branch_a, w_branch_b, w_branch_c, w_out, xattn_norm, mem_norm, xattn_wq, xattn_wkv, xattn_wo, ffn2_norm, ffn2_w13, ffn2_w2, final_norm):
    p = {
        'ffn1_norm': ffn1_norm, 'ffn1_w13': ffn1_w13, 'ffn1_w2': ffn1_w2,
        'mix_norm': mix_norm, 'w_in': w_in, 'w_gate': w_gate, 'b_gate': b_gate,
        'sgu_ln_g': sgu_ln_g, 'sgu_ln_b': sgu_ln_b, 'sgu_ws': sgu_ws, 'sgu_bias': sgu_bias,
        'pool_w': pool_w, 'pool_scale': pool_scale,
        'conv_w': conv_w, 'conv_b': conv_b, 'dt_bias': dt_bias, 'a_log': a_log, 'd_skip': d_skip, 'ssd_norm': ssd_norm,
        'w_branch_a': w_branch_a, 'w_branch_b': w_branch_b, 'w_branch_c': w_branch_c, 'w_out': w_out,
        'xattn_norm': xattn_norm, 'mem_norm': mem_norm, 'xattn_wq': xattn_wq, 'xattn_wkv': xattn_wkv, 'xattn_wo': xattn_wo,
        'ffn2_norm': ffn2_norm, 'ffn2_w13': ffn2_w13, 'ffn2_w2': ffn2_w2,
        'final_norm': final_norm,
    }
    y_prompt = encoder_trunk(x_prompt, mem_prompt, p)
    y_sample = encoder_trunk(x_sample, mem_sample, p)
    return (y_prompt, y_sample)
```

```python
import functools

import numpy as np
import jax
import jax.numpy as jnp
from jax import lax
from jax.experimental import pallas as pl
from jax.experimental.pallas import tpu as pltpu

F32 = jnp.float32
BF16 = jnp.bfloat16

D = 1024
N_MEM = 256
EPS = 1e-6
D_FF = 2816
SGU_W = 512
SGU_G = 4
CHUNK = 128
POOL_W = 512
POOL_WINDOWS = (2, 4, 8, 16)
SSD_INNER = 1024
HEADDIM = 64
HEADS = 16
GROUPS = 2
STATE = 64
GN = GROUPS * STATE
CONV_DIM = SSD_INNER + 2 * GN
CONV_K = 4
CONV_LEFT = 2
XA_HEADS = 4
XA_HD = D // XA_HEADS
LANES = 128
HALO = 16

TM_FFN = 512
TM_MIXIN = 512
TS_STATE = 512
TS_MIX = 512
TM_ATTN = 512
FFN_FC = 256

VMEM_LIMIT = 56 * 1024 * 1024


def _cparams(n_axes):
    return pltpu.CompilerParams(dimension_semantics=("arbitrary",) * n_axes,
                                vmem_limit_bytes=VMEM_LIMIT)


def _const_spec(shape):
    nd = len(shape)
    return pl.BlockSpec(shape, lambda *_: (0,) * nd, pipeline_mode=pl.Buffered(1))


def _rms(x, g):
    return x * lax.rsqrt(jnp.mean(x * x, axis=-1, keepdims=True) + EPS) * g


def _sigmoid(x):
    return 1.0 / (1.0 + jnp.exp(-x))


def _split3(x):
    hi = x.astype(BF16)
    r1 = x - hi.astype(F32)
    mid = r1.astype(BF16)
    lo = (r1 - mid.astype(F32)).astype(BF16)
    return hi, mid, lo


def _dot(a, b):
    return jnp.dot(a, b, preferred_element_type=F32)


def _dot_exact01(m01, x):
    hi, mid, lo = _split3(x)
    return _dot(m01, hi) + _dot(m01, mid) + _dot(m01, lo)


def _ffn_body(x_ref, g_ref, w13_ref, w2_ref, fin_ref, o_ref, gbuf_ref, *, final):
    x = x_ref[...]
    xn = _rms(x, g_ref[...]).astype(BF16)
    for c in range(D_FF // FFN_FC):
        a = _dot(xn, w13_ref[:, c * FFN_FC:(c + 1) * FFN_FC])
        b = _dot(xn, w13_ref[:, D_FF + c * FFN_FC:D_FF + (c + 1) * FFN_FC])
        gbuf_ref[:, c * FFN_FC:(c + 1) * FFN_FC] = (a * _sigmoid(a) * b).astype(BF16)
    y = x + 0.5 * _dot(gbuf_ref[...], w2_ref[...])
    if final:
        y = _rms(y, fin_ref[...])
    o_ref[...] = y


def _ffn(x, g, w13, w2, fin, final):
    t = x.shape[0]
    tm = TM_FFN
    return pl.pallas_call(
        functools.partial(_ffn_body, final=final),
        grid=(t // tm,),
        in_specs=[pl.BlockSpec((tm, D), lambda i: (i, 0)),
                  _const_spec((1, D)), _const_spec((D, 2 * D_FF)), _const_spec((D_FF, D)),
                  _const_spec((1, D))],
        out_specs=pl.BlockSpec((tm, D), lambda i: (i, 0)),
        out_shape=jax.ShapeDtypeStruct((t, D), F32),
        scratch_shapes=[pltpu.VMEM((tm, D_FF), BF16)],
        compiler_params=_cparams(1),
        name="ffn_final" if final else "ffn",
    )(x, g, w13, w2, fin)


def _mixin_body(x_ref, xp_ref, xn_ref, g_ref, wuv_ref, wpin_ref, wz_ref, wxbc_ref, wdt_ref,
                wgate_ref, bgate_ref, lng_ref, lnb_ref, ws_ref, sbias_ref, poolw_ref, pscale_ref,
                convw_ref, convb_ref, dtb_ref, wa_ref, wb_ref,
                ab_ref, g2_ref, z_ref, xc_ref, dt_ref,
                pin_ext, xbc_ext, a_buf, b_buf, *, seq_len):
    tm = x_ref.shape[1]
    t = pl.program_id(1)
    nt = pl.num_programs(1)
    g = g_ref[...]
    xp = jnp.where(t > 0, xp_ref[0], 0.0)
    xn = jnp.where(t < nt - 1, xn_ref[0], 0.0)
    n_main = _rms(x_ref[0], g).astype(BF16)
    n_prev = _rms(xp, g).astype(BF16)
    n_next = _rms(xn, g).astype(BF16)

    wpin = wpin_ref[...]
    wxbc = wxbc_ref[...]
    pin_ext[0:HALO, :] = _dot(n_prev, wpin)
    pin_ext[HALO:HALO + tm, :] = _dot(n_main, wpin)
    pin_ext[HALO + tm:2 * HALO + tm, :] = _dot(n_next, wpin)
    xbc_ext[0:HALO, :] = _dot(n_prev, wxbc)
    xbc_ext[HALO:HALO + tm, :] = _dot(n_main, wxbc)
    xbc_ext[HALO + tm:2 * HALO + tm, :] = _dot(n_next, wxbc)

    acc = convb_ref[...] + xbc_ext[pl.ds(HALO - CONV_LEFT, tm), :] * convw_ref[0:1, :]
    for k in range(1, CONV_K):
        acc = acc + xbc_ext[pl.ds(HALO - CONV_LEFT + k, tm), :] * convw_ref[k:k + 1, :]
    xc_ref[0] = (acc * _sigmoid(acc)).astype(BF16)

    raw = _dot(n_main, wdt_ref[...]) + dtb_ref[...]
    sp = jnp.maximum(raw, 0.0) + jnp.log(1.0 + jnp.exp(-jnp.abs(raw)))
    lane = lax.broadcasted_iota(jnp.int32, (tm, LANES), 1)
    dt_ref[0] = jnp.where(lane < 2 * HEADS, sp, 0.0)

    z_ref[0] = _dot(n_main, wz_ref[...]).astype(BF16)

    pos = t * tm + lax.broadcasted_iota(jnp.int32, (tm, LANES), 0)
    for gi, w in enumerate(POOL_WINDOWS):
        cs = slice(gi * LANES, (gi + 1) * LANES)
        s = pin_ext[pl.ds(HALO - w // 2, tm), cs]
        for k in range(1, w):
            s = s + pin_ext[pl.ds(HALO - w // 2 + k, tm), cs]
        lo = jnp.clip(pos - w // 2, 0, seq_len)
        hi = jnp.clip(pos + w - w // 2, 0, seq_len)
        cnt = (hi - lo).astype(F32)
        pooled = (s / cnt - pin_ext[HALO:HALO + tm, cs]).astype(BF16)
        b_buf[:, cs] = (_dot(pooled, poolw_ref[gi]) * pscale_ref[:, cs]).astype(BF16)

    uv = _dot(n_main, wuv_ref[...])

    def gelu(v):
        return 0.5 * v * (1.0 + jnp.tanh(0.7978845608028654 * (v + 0.044715 * (v * v * v))))

    u = gelu(uv[:, :SGU_W])
    v = gelu(uv[:, SGU_W:])
    vc = v - jnp.mean(v, axis=-1, keepdims=True)
    var = jnp.mean(vc * vc, axis=-1, keepdims=True)
    vn = (vc * lax.rsqrt(var + EPS) * lng_ref[...] + lnb_ref[...]).astype(BF16)
    for c in range(tm // CHUNK):
        rs = slice(c * CHUNK, (c + 1) * CHUNK)
        for gi in range(SGU_G):
            cs = slice(gi * LANES, (gi + 1) * LANES)
            mixed = _dot(ws_ref[gi], vn[rs, cs]) + sbias_ref[gi]
            a_buf[rs, cs] = (u[rs, cs] * mixed).astype(BF16)

    gates = _sigmoid(_dot(n_main, wgate_ref[...]) + bgate_ref[...])
    ab = gates[:, :D] * _dot(a_buf[...], wa_ref[...]) + gates[:, D:2 * D] * _dot(b_buf[...], wb_ref[...])
    ab_ref[0] = ab.astype(BF16)
    g2_ref[0] = gates[:, 2 * D:].astype(BF16)


def _mixin(x, p, l):
    b, s, _ = x.shape
    tm = TM_MIXIN
    nt = s // tm
    hb = tm // HALO
    tile = lambda w: pl.BlockSpec((1, tm, w), lambda i, j: (i, j, 0))
    out_shapes = [jax.ShapeDtypeStruct((b, s, D), BF16),
                  jax.ShapeDtypeStruct((b, s, D), BF16),
                  jax.ShapeDtypeStruct((b, s, SSD_INNER), BF16),
                  jax.ShapeDtypeStruct((b, s, CONV_DIM), BF16),
                  jax.ShapeDtypeStruct((b, s, LANES), F32)]
    consts = [p["mix_norm"][l], p["w_uv"][l], p["w_pin"][l], p["w_z"][l], p["w_xbc"][l], p["w_dt"][l],
              p["w_gate"][l], p["b_gate"][l], p["sgu_ln_g"][l], p["sgu_ln_b"][l], p["sgu_ws"][l],
              p["sgu_bias"][l], p["pool_w"][l], p["pool_scale"][l], p["conv_w"][l], p["conv_b"][l],
              p["dt_bias"][l], p["w_branch_a"][l], p["w_branch_b"][l]]
    return pl.pallas_call(
        functools.partial(_mixin_body, seq_len=s),
        grid=(b, nt),
        in_specs=[tile(D),
                  pl.BlockSpec((1, HALO, D), lambda i, j: (i, jnp.maximum(j * hb - 1, 0), 0)),
                  pl.BlockSpec((1, HALO, D), lambda i, j: (i, jnp.minimum((j + 1) * hb, nt * hb - 1), 0))]
                 + [_const_spec(c.shape) for c in consts],
        out_specs=[tile(D), tile(D), tile(SSD_INNER), tile(CONV_DIM), tile(LANES)],
        out_shape=out_shapes,
        scratch_shapes=[pltpu.VMEM((tm + 2 * HALO, POOL_W), F32),
                        pltpu.VMEM((tm + 2 * HALO, CONV_DIM), F32),
                        pltpu.VMEM((tm, SGU_W), BF16),
                        pltpu.VMEM((tm, POOL_W), BF16)],
        compiler_params=_cparams(2),
        name="mixin",
    )(x, x, x, *consts)


def _state_body(xc_ref, dt_ref, alog_ref, tri_ref, e_ref, o_ref, h_ref):
    d = pl.program_id(0)
    t = pl.program_id(2)
    nc = xc_ref.shape[1] // CHUNK

    @pl.when(t == 0)
    def _():
        h_ref[...] = jnp.zeros_like(h_ref)

    a_row = -jnp.exp(alog_ref[0, 0:1, :]) * alog_ref[0, 1:2, :]
    tri = tri_ref[0]
    e = e_ref[0]
    row = lax.broadcasted_iota(jnp.int32, (GN, SSD_INNER // GROUPS), 0)
    for ci in range(nc):
        c = jnp.where(d == 0, ci, nc - 1 - ci)
        r0 = pl.multiple_of(c * CHUNK, CHUNK)
        xcq = xc_ref[0, pl.ds(r0, CHUNK), :]
        xh = xcq[:, :SSD_INNER].astype(F32)
        bm = xcq[:, SSD_INNER:SSD_INNER + GN].astype(F32)
        dt = dt_ref[0, pl.ds(r0, CHUNK), :]
        da = dt * a_row
        rest = _dot_exact01(tri, da)
        w = dt * jnp.exp(rest)
        cd = jnp.exp(jnp.sum(da, axis=0, keepdims=True))
        whi = w.astype(BF16)
        wlo = (w - whi.astype(F32)).astype(BF16)
        wexp = _dot(whi, e) + _dot(wlo, e)
        cd8 = jnp.broadcast_to(cd, (8, LANES))
        chi = cd8.astype(BF16)
        clo = (cd8 - chi.astype(F32)).astype(BF16)
        cdexp = (_dot(chi, e) + _dot(clo, e))[0:1, :]
        xw = (xh * wexp).astype(BF16)
        bmt = bm.T.astype(BF16)
        half = SSD_INNER // GROUPS
        st = jnp.concatenate([_dot(bmt[0:STATE, :], xw[:, :half]),
                              _dot(bmt[STATE:, :], xw[:, half:])], axis=0)
        h = h_ref[...]
        o_ref[0, 0, c] = h.astype(BF16)
        cdmat = jnp.where(row < STATE, cdexp[:, :half], cdexp[:, half:])
        h_ref[...] = h * cdmat + st


def _state(xc, dt, p, l):
    b, s, _ = xc.shape
    ts = TS_STATE
    nt = s // ts
    nc = ts // CHUNK
    tsel = lambda d, j: jnp.where(d == 0, j, nt - 1 - j)
    return pl.pallas_call(
        _state_body,
        grid=(2, b, nt),
        in_specs=[pl.BlockSpec((1, ts, CONV_DIM), lambda d, i, j: (i, tsel(d, j), 0)),
                  pl.BlockSpec((1, ts, LANES), lambda d, i, j: (i, tsel(d, j), 0)),
                  pl.BlockSpec((1, 2, LANES), lambda d, i, j: (d, 0, 0)),
                  pl.BlockSpec((1, CHUNK, CHUNK), lambda d, i, j: (d, 0, 0)),
                  pl.BlockSpec((1, LANES, SSD_INNER), lambda d, i, j: (d, 0, 0))],
        out_specs=pl.BlockSpec((1, 1, nc, GN, SSD_INNER // GROUPS),
                               lambda d, i, j: (d, i, tsel(d, j), 0, 0)),
        out_shape=jax.ShapeDtypeStruct((2, b, s // CHUNK, GN, SSD_INNER // GROUPS), BF16),
        scratch_shapes=[pltpu.VMEM((GN, SSD_INNER // GROUPS), F32)],
        compiler_params=_cparams(3),
        name="ssd_state",
    )(xc, dt, p["alog_dir"][l], p["tri_strict"], p["expand_dir"])


def _mix_body(x_ref, ab_ref, g2_ref, z_ref, xc_ref, dt_ref, prev_ref, alog_ref, tri_ref, e2_ref,
              dskip_ref, normg_ref, wc_ref, wout_ref, o_ref, cin_ref):
    ts = x_ref.shape[1]
    nc = ts // CHUNK
    a_row = -jnp.exp(alog_ref[0:1, :]) * alog_ref[1:2, :]
    tri_lo = tri_ref[0]
    tri_up = tri_ref[1]
    e2 = e2_ref[...]
    lane = lax.broadcasted_iota(jnp.int32, (CHUNK, LANES), 1)
    rowi = lax.broadcasted_iota(jnp.int32, (CHUNK, LANES), 0)
    lower = lane <= rowi
    upper = lane >= rowi
    half = SSD_INNER // GROUPS
    rowp = lax.broadcasted_iota(jnp.int32, (GN, half), 0)
    neg_inf = jnp.float32(-jnp.inf)

    def chunk(c, carry):
        r0 = pl.multiple_of(c * CHUNK, CHUNK)
        xcq = xc_ref[0, pl.ds(r0, CHUNK), :]
        xh = xcq[:, :SSD_INNER]
        bm = xcq[:, SSD_INNER:SSD_INNER + GN]
        cm = xcq[:, SSD_INNER + GN:]
        dt = dt_ref[0, pl.ds(r0, CHUNK), :]
        da = dt * a_row
        hi, mid, lo = _split3(da)
        pre = _dot(tri_lo, hi) + _dot(tri_lo, mid) + _dot(tri_lo, lo)
        suf = _dot(tri_up, hi) + _dot(tri_up, mid) + _dot(tri_up, lo)
        pcol = jnp.where(lane < HEADS, pre, suf)
        prow = pcol.T
        dtrow = dt.T

        cbs = []
        cmf = cm.astype(F32)
        for g in range(GROUPS):
            cmg = jnp.where((lane >= g * STATE) & (lane < (g + 1) * STATE), cmf, 0.0).astype(BF16)
            cbs.append(lax.dot_general(cmg, bm, (((1,), (1,)), ((), ())), preferred_element_type=F32))

        ydiag = []
        for j in range(HEADS // 2):
            g = (2 * j) // (HEADS // GROUPS)
            ms = []
            for h in (2 * j, 2 * j + 1):
                colf = jnp.sum(jnp.where(lane == h, pcol, 0.0), axis=1, keepdims=True)
                colb = jnp.sum(jnp.where(lane == HEADS + h, pcol, 0.0), axis=1, keepdims=True)
                lf = jnp.exp(jnp.where(lower, colf - prow[h:h + 1, :], neg_inf)) * dtrow[h:h + 1, :]
                lb = jnp.exp(jnp.where(upper, colb - prow[HEADS + h:HEADS + h + 1, :], neg_inf)) \
                    * dtrow[HEADS + h:HEADS + h + 1, :]
                ms.append((cbs[g] * (lf + lb)).astype(BF16))
            xp = xh[:, j * LANES:(j + 1) * LANES].astype(F32)
            rhs = jnp.concatenate([jnp.where(lane < HEADDIM, xp, 0.0).astype(BF16),
                                   jnp.where(lane >= HEADDIM, xp, 0.0).astype(BF16)], axis=0)
            ydiag.append(_dot(jnp.concatenate(ms, axis=1), rhs))
        y = jnp.concatenate(ydiag, axis=1)

        ee = jnp.exp(pcol)
        ehi = ee.astype(BF16)
        elo = (ee - ehi.astype(F32)).astype(BF16)
        eexp = _dot(ehi, e2) + _dot(elo, e2)
        for dirn in range(2):
            pv = prev_ref[dirn, 0, c].astype(F32)
            pbd = jnp.concatenate([jnp.where(rowp < STATE, pv, 0.0).astype(BF16),
                                   jnp.where(rowp >= STATE, pv, 0.0).astype(BF16)], axis=1)
            y = y + _dot(cm, pbd) * eexp[:, dirn * SSD_INNER:(dirn + 1) * SSD_INNER]

        y = y + xh.astype(F32) * dskip_ref[...]
        z = z_ref[0, pl.ds(r0, CHUNK), :].astype(F32)
        y = y * (z * _sigmoid(z))
        outs = []
        for g in range(GROUPS):
            yg = y[:, g * half:(g + 1) * half]
            outs.append(_rms(yg, normg_ref[:, g * half:(g + 1) * half]))
        cin_ref[pl.ds(r0, CHUNK), :] = jnp.concatenate(outs, axis=1).astype(BF16)
        return carry

    lax.fori_loop(0, nc, chunk, 0)

    cproj = _dot(cin_ref[...], wc_ref[...])
    merged = ab_ref[0].astype(F32) + g2_ref[0].astype(F32) * cproj
    o_ref[0] = x_ref[0] + _dot(merged.astype(BF16), wout_ref[...])


def _mix(x, ab, g2, z, xc, dt, prev, p, l):
    b, s, _ = x.shape
    ts = TS_MIX
    nc = ts // CHUNK
    tile = lambda w: pl.BlockSpec((1, ts, w), lambda i, j: (i, j, 0))
    consts = [p["alog_all"][l], p["tri_incl"], p["expand_all"], p["d_skip"][l], p["ssd_norm"][l],
              p["w_branch_c"][l], p["w_out"][l]]
    return pl.pallas_call(
        _mix_body,
        grid=(b, s // ts),
        in_specs=[tile(D), tile(D), tile(D), tile(SSD_INNER), tile(CONV_DIM), tile(LANES),
                  pl.BlockSpec((2, 1, nc, GN, SSD_INNER // GROUPS), lambda i, j: (0, i, j, 0, 0))]
                 + [_const_spec(c.shape) for c in consts],
        out_specs=tile(D),
        out_shape=jax.ShapeDtypeStruct((b, s, D), F32),
        scratch_shapes=[pltpu.VMEM((ts, SSD_INNER), BF16)],
        compiler_params=_cparams(2),
        name="ssd_mix",
    )(x, ab, g2, z, xc, dt, prev, *consts)


def _kv_body(m_ref, g_ref, wkv_ref, k_ref, v_ref):
    mn = _rms(m_ref[0], g_ref[...]).astype(BF16)
    kv = _dot(mn, wkv_ref[...])
    k_ref[0] = kv[:, :D].astype(BF16)
    v_ref[0] = kv[:, D:].astype(BF16)


def _kv(mem, p, l):
    b, m, _ = mem.shape
    blk = pl.BlockSpec((1, m, D), lambda i: (i, 0, 0))
    return pl.pallas_call(
        _kv_body,
        grid=(b,),
        in_specs=[blk, _const_spec((1, D)), _const_spec((D, 2 * D))],
        out_specs=[blk, blk],
        out_shape=[jax.ShapeDtypeStruct((b, m, D), BF16)] * 2,
        compiler_params=_cparams(1),
        name="xattn_kv",
    )(mem, p["mem_norm"][l], p["xattn_wkv"][l])


def _attn_body(x_ref, k_ref, v_ref, g_ref, wq_ref, wo_ref, o_ref, obuf_ref):
    x = x_ref[0]
    xn = _rms(x, g_ref[...]).astype(BF16)
    q = (_dot(xn, wq_ref[...]) * (XA_HD ** -0.5)).astype(BF16)
    for h in range(XA_HEADS):
        cs = slice(h * XA_HD, (h + 1) * XA_HD)
        s = lax.dot_general(q[:, cs], k_ref[0, :, cs], (((1,), (1,)), ((), ())),
                            preferred_element_type=F32)
        e = jnp.exp(s - jnp.max(s, axis=-1, keepdims=True))
        pr = (e * (1.0 / jnp.sum(e, axis=-1, keepdims=True))).astype(BF16)
        obuf_ref[:, cs] = _dot(pr, v_ref[0, :, cs]).astype(BF16)
    o_ref[0] = x + _dot(obuf_ref[...], wo_ref[...])


def _attn(x, k, v, p, l):
    b, s, _ = x.shape
    m = k.shape[1]
    tm = TM_ATTN
    tile = pl.BlockSpec((1, tm, D), lambda i, j: (i, j, 0))
    kvb = pl.BlockSpec((1, m, D), lambda i, j: (i, 0, 0))
    return pl.pallas_call(
        _attn_body,
        grid=(b, s // tm),
        in_specs=[tile, kvb, kvb, _const_spec((1, D)), _const_spec((D, D)), _const_spec((D, D))],
        out_specs=tile,
        out_shape=jax.ShapeDtypeStruct((b, s, D), F32),
        scratch_shapes=[pltpu.VMEM((tm, D), BF16)],
        compiler_params=_cparams(2),
        name="xattn",
    )(x, k, v, p["xattn_norm"][l], p["xattn_wq"][l], p["xattn_wo"][l])


def _constants():
    q = np.arange(CHUNK)
    lower_incl = (q[None, :] <= q[:, None]).astype(np.float32)
    upper_incl = (q[None, :] >= q[:, None]).astype(np.float32)
    strict_up = (q[None, :] > q[:, None]).astype(np.float32)
    strict_lo = (q[None, :] < q[:, None]).astype(np.float32)
    e_f = np.zeros((LANES, SSD_INNER), np.float32)
    e_b = np.zeros((LANES, SSD_INNER), np.float32)
    for h in range(HEADS):
        e_f[h, h * HEADDIM:(h + 1) * HEADDIM] = 1.0
        e_b[HEADS + h, h * HEADDIM:(h + 1) * HEADDIM] = 1.0
    return dict(
        tri_incl=jnp.asarray(np.stack([lower_incl, upper_incl]), BF16),
        tri_strict=jnp.asarray(np.stack([strict_up, strict_lo]), BF16),
        expand_dir=jnp.asarray(np.stack([e_f, e_b]), BF16),
        expand_all=jnp.asarray(np.concatenate([e_f, e_b], axis=1), BF16),
    )


def _prepare(raw):
    L = raw["w_in"].shape[0]
    w_in = raw["w_in"]
    o_pin = 2 * SGU_W
    o_z = o_pin + POOL_W
    o_xbc = o_z + SSD_INNER
    o_dt = o_xbc + CONV_DIM
    row = lambda a: a.reshape(L, 1, -1).astype(F32)
    p = dict(_constants())
    for k in ("ffn1_norm", "ffn2_norm", "mix_norm", "xattn_norm", "mem_norm", "b_gate", "sgu_ln_g", "sgu_ln_b",
              "pool_scale", "conv_b", "ssd_norm"):
        p[k] = row(raw[k])
    p["final_norm"] = raw["final_norm"].reshape(1, D).astype(F32)
    for k in ("ffn1_w13", "ffn1_w2", "ffn2_w13", "ffn2_w2", "w_gate", "sgu_ws", "pool_w", "w_branch_a",
              "w_branch_b", "w_branch_c", "w_out", "xattn_wq", "xattn_wkv", "xattn_wo"):
        p[k] = raw[k].astype(BF16)
    p["w_uv"] = w_in[:, :, :o_pin].astype(BF16)
    p["w_pin"] = w_in[:, :, o_pin:o_z].astype(BF16)
    p["w_z"] = w_in[:, :, o_z:o_xbc].astype(BF16)
    p["w_xbc"] = w_in[:, :, o_xbc:o_dt].astype(BF16)
    p["w_dt"] = jnp.pad(w_in[:, :, o_dt:], ((0, 0), (0, 0), (0, LANES - 2 * HEADS))).astype(BF16)
    p["dt_bias"] = jnp.pad(raw["dt_bias"].reshape(L, 1, 2 * HEADS), ((0, 0), (0, 0), (0, LANES - 2 * HEADS))).astype(F32)
    p["conv_w"] = raw["conv_w"].astype(F32)
    p["sgu_bias"] = jnp.broadcast_to(raw["sgu_bias"][:, :, :, None], (L, SGU_G, CHUNK, LANES)).astype(F32)
    alog = raw["a_log"].astype(F32)
    zeros = jnp.zeros((L, HEADS), F32)
    ones = jnp.ones((L, HEADS), F32)
    padw = LANES - 2 * HEADS
    def rows(vals, mask):
        v = jnp.pad(jnp.concatenate(vals, axis=-1), ((0, 0), (0, padw)))
        m = jnp.pad(jnp.concatenate(mask, axis=-1), ((0, 0), (0, padw)))
        return jnp.stack([v, m], axis=1)
    p["alog_all"] = rows([alog[:, 0], alog[:, 1]], [ones, ones])
    p["alog_dir"] = jnp.stack([rows([alog[:, 0], zeros], [ones, zeros]),
                               rows([zeros, alog[:, 1]], [zeros, ones])], axis=1)
    p["d_skip"] = jnp.repeat(raw["d_skip"].astype(F32), HEADDIM, axis=-1).reshape(L, 1, SSD_INNER)
    return p, L


def _trunk(x, mem, p, depth):
    b, s, _ = x.shape
    for l in range(depth):
        xf = _ffn(x.reshape(b * s, D), p["ffn1_norm"][l], p["ffn1_w13"][l], p["ffn1_w2"][l],
                  p["final_norm"], False)
        x = xf.reshape(b, s, D)
        ab, g2, z, xc, dt = _mixin(x, p, l)
        prev = _state(xc, dt, p, l)
        x = _mix(x, ab, g2, z, xc, dt, prev, p, l)
        k, v = _kv(mem, p, l)
        x = _attn(x, k, v, p, l)
        xf = _ffn(x.reshape(b * s, D), p["ffn2_norm"][l], p["ffn2_w13"][l], p["ffn2_w2"][l],
                  p["final_norm"], l == depth - 1)
        x = xf.reshape(b, s, D)
    return x


def kernel(x_prompt, x_sample, mem_prompt, mem_sample, ffn1_norm, ffn1_w13, ffn1_w2, mix_norm, w_in, w_gate, b_gate, sgu_ln_g, sgu_ln_b, sgu_ws, sgu_bias, pool_w, pool_scale, conv_w, conv_b, dt_bias, a_log, d_skip, ssd_norm, w_branch_a, w_branch_b, w_branch_c, w_out, xattn_norm, mem_norm, xattn_wq, xattn_wkv, xattn_wo, ffn2_norm, ffn2_w13, ffn2_w2, final_norm):
    raw = dict(ffn1_norm=ffn1_norm, ffn1_w13=ffn1_w13, ffn1_w2=ffn1_w2, mix_norm=mix_norm, w_in=w_in,
               w_gate=w_gate, b_gate=b_gate, sgu_ln_g=sgu_ln_g, sgu_ln_b=sgu_ln_b, sgu_ws=sgu_ws,
               sgu_bias=sgu_bias, pool_w=pool_w, pool_scale=pool_scale, conv_w=conv_w, conv_b=conv_b,
               dt_bias=dt_bias, a_log=a_log, d_skip=d_skip, ssd_norm=ssd_norm, w_branch_a=w_branch_a,
               w_branch_b=w_branch_b, w_branch_c=w_branch_c, w_out=w_out, xattn_norm=xattn_norm,
               mem_norm=mem_norm, xattn_wq=xattn_wq, xattn_wkv=xattn_wkv, xattn_wo=xattn_wo,
               ffn2_norm=ffn2_norm, ffn2_w13=ffn2_w13, ffn2_w2=ffn2_w2, final_norm=final_norm)
    p, depth = _prepare(raw)
    return (_trunk(x_prompt, mem_prompt, p, depth), _trunk(x_sample, mem_sample, p, depth))
```

```python
import functools

import numpy as np
import jax
import jax.numpy as jnp
from jax import lax
from jax.experimental import pallas as pl
from jax.experimental.pallas import tpu as pltpu

F32 = jnp.float32
BF16 = jnp.bfloat16

D = 1024
N_MEM = 256
EPS = 1e-6
D_FF = 2816
SGU_W = 512
SGU_G = 4
CHUNK = 128
POOL_W = 512
POOL_WINDOWS = (2, 4, 8, 16)
SSD_INNER = 1024
HEADDIM = 64
HEADS = 16
GROUPS = 2
STATE = 64
GN = GROUPS * STATE
CONV_DIM = SSD_INNER + 2 * GN
CONV_K = 4
CONV_LEFT = 2
XA_HEADS = 4
XA_HD = D // XA_HEADS
LANES = 128
LOG2E = 1.4426950408889634
HALO = 16

TM_FFN = 512
TM_MIXIN = 512
TS_STATE = 512
TS_MIX = 512
TM_ATTN = 512
FFN_FC = 256

VMEM_LIMIT = 56 * 1024 * 1024


def _cparams(n_axes):
    return pltpu.CompilerParams(dimension_semantics=("arbitrary",) * n_axes,
                                vmem_limit_bytes=VMEM_LIMIT)


def _const_spec(shape):
    nd = len(shape)
    return pl.BlockSpec(shape, lambda *_: (0,) * nd, pipeline_mode=pl.Buffered(1))


def _layer_spec(stacked, l):
    nd = stacked.ndim - 1
    return pl.BlockSpec((None,) + stacked.shape[1:], lambda *_: (l,) + (0,) * nd,
                        pipeline_mode=pl.Buffered(1))


def _rms(x, g):
    return x * lax.rsqrt(jnp.mean(x * x, axis=-1, keepdims=True) + EPS) * g


def _sigmoid(x):
    return 1.0 / (1.0 + jnp.exp(-x))


def _split3(x):
    hi = x.astype(BF16)
    r1 = x - hi.astype(F32)
    mid = r1.astype(BF16)
    lo = (r1 - mid.astype(F32)).astype(BF16)
    return hi, mid, lo


def _dot(a, b):
    return jnp.dot(a, b, preferred_element_type=F32)


def _ffn_body(x_ref, g_ref, w13_ref, w2_ref, fin_ref, o_ref, gbuf_ref, *, final):
    x = x_ref[...]
    xn = _rms(x, g_ref[...]).astype(BF16)
    for c in range(D_FF // FFN_FC):
        a = _dot(xn, w13_ref[:, c * FFN_FC:(c + 1) * FFN_FC])
        b = _dot(xn, w13_ref[:, D_FF + c * FFN_FC:D_FF + (c + 1) * FFN_FC])
        gbuf_ref[:, c * FFN_FC:(c + 1) * FFN_FC] = (a * _sigmoid(a) * b).astype(BF16)
    y = x + 0.5 * _dot(gbuf_ref[...], w2_ref[...])
    if final:
        y = _rms(y, fin_ref[...])
    o_ref[...] = y


def _ffn(x, g, w13, w2, fin, final, l):
    t = x.shape[0]
    tm = TM_FFN
    return pl.pallas_call(
        functools.partial(_ffn_body, final=final),
        grid=(t // tm,),
        in_specs=[pl.BlockSpec((tm, D), lambda i: (i, 0)),
                  _layer_spec(g, l), _layer_spec(w13, l), _layer_spec(w2, l), _const_spec((1, D))],
        out_specs=pl.BlockSpec((tm, D), lambda i: (i, 0)),
        out_shape=jax.ShapeDtypeStruct((t, D), F32),
        scratch_shapes=[pltpu.VMEM((tm, D_FF), BF16)],
        compiler_params=_cparams(1),
        name="ffn_final" if final else "ffn",
    )(x, g, w13, w2, fin)


def _mixin_body(x_ref, xp_ref, xn_ref, g_ref, wuv_ref, wpin_ref, wz_ref, wxbc_ref, wdt_ref,
                wgate_ref, bgate_ref, lng_ref, lnb_ref, ws_ref, sbias_ref, poolw_ref, pscale_ref,
                convw_ref, convb_ref, dtb_ref, wa_ref, wb_ref,
                ab_ref, g2_ref, z_ref, xc_ref, dt_ref,
                pin_ext, xbc_ext, a_buf, b_buf, *, seq_len):
    tm = x_ref.shape[1]
    t = pl.program_id(1)
    nt = pl.num_programs(1)
    g = g_ref[...]
    xp = jnp.where(t > 0, xp_ref[0], 0.0)
    xn = jnp.where(t < nt - 1, xn_ref[0], 0.0)
    n_main = _rms(x_ref[0], g).astype(BF16)
    n_prev = _rms(xp, g).astype(BF16)
    n_next = _rms(xn, g).astype(BF16)

    n_ext = jnp.concatenate([n_prev, n_main, n_next], axis=0)
    pin_ext[...] = _dot(n_ext, wpin_ref[...])
    xbc_ext[...] = _dot(n_ext, wxbc_ref[...])

    acc = convb_ref[...] + xbc_ext[pl.ds(HALO - CONV_LEFT, tm), :] * convw_ref[0:1, :]
    for k in range(1, CONV_K):
        acc = acc + xbc_ext[pl.ds(HALO - CONV_LEFT + k, tm), :] * convw_ref[k:k + 1, :]
    xc_ref[0] = (acc * _sigmoid(acc)).astype(BF16)

    raw = _dot(n_main, wdt_ref[...]) + dtb_ref[...]
    sp = jnp.maximum(raw, 0.0) + jnp.log(1.0 + jnp.exp(-jnp.abs(raw)))
    lane = lax.broadcasted_iota(jnp.int32, (tm, LANES), 1)
    dt_ref[0] = jnp.where(lane < 2 * HEADS, sp, 0.0)

    z_ref[0] = _dot(n_main, wz_ref[...]).astype(BF16)

    pos = t * tm + lax.broadcasted_iota(jnp.int32, (tm, LANES), 0)
    for gi, w in enumerate(POOL_WINDOWS):
        cs = slice(gi * LANES, (gi + 1) * LANES)
        s = pin_ext[pl.ds(HALO - w // 2, tm), cs]
        for k in range(1, w):
            s = s + pin_ext[pl.ds(HALO - w // 2 + k, tm), cs]
        lo = jnp.clip(pos - w // 2, 0, seq_len)
        hi = jnp.clip(pos + w - w // 2, 0, seq_len)
        cnt = (hi - lo).astype(F32)
        pooled = (s / cnt - pin_ext[HALO:HALO + tm, cs]).astype(BF16)
        b_buf[:, cs] = (_dot(pooled, poolw_ref[gi]) * pscale_ref[:, cs]).astype(BF16)

    uv = _dot(n_main, wuv_ref[...])

    def gelu(v):
        return 0.5 * v * (1.0 + jnp.tanh(0.7978845608028654 * (v + 0.044715 * (v * v * v))))

    u = gelu(uv[:, :SGU_W])
    v = gelu(uv[:, SGU_W:])
    vc = v - jnp.mean(v, axis=-1, keepdims=True)
    var = jnp.mean(vc * vc, axis=-1, keepdims=True)
    vn = (vc * lax.rsqrt(var + EPS) * lng_ref[...] + lnb_ref[...]).astype(BF16)
    for c in range(tm // CHUNK):
        rs = slice(c * CHUNK, (c + 1) * CHUNK)
        for gi in range(SGU_G):
            cs = slice(gi * LANES, (gi + 1) * LANES)
            mixed = _dot(ws_ref[gi], vn[rs, cs]) + sbias_ref[gi]
            a_buf[rs, cs] = (u[rs, cs] * mixed).astype(BF16)

    gates = _sigmoid(_dot(n_main, wgate_ref[...]) + bgate_ref[...])
    ab = gates[:, :D] * _dot(a_buf[...], wa_ref[...]) + gates[:, D:2 * D] * _dot(b_buf[...], wb_ref[...])
    ab_ref[0] = ab.astype(BF16)
    g2_ref[0] = gates[:, 2 * D:].astype(BF16)


def _mixin(x, p, l):
    b, s, _ = x.shape
    tm = TM_MIXIN
    nt = s // tm
    hb = tm // HALO
    tile = lambda w: pl.BlockSpec((1, tm, w), lambda i, j: (i, j, 0))
    out_shapes = [jax.ShapeDtypeStruct((b, s, D), BF16),
                  jax.ShapeDtypeStruct((b, s, D), BF16),
                  jax.ShapeDtypeStruct((b, s, SSD_INNER), BF16),
                  jax.ShapeDtypeStruct((b, s, CONV_DIM), BF16),
                  jax.ShapeDtypeStruct((b, s, LANES), F32)]
    consts = [p[k] for k in ("mix_norm", "w_uv", "w_pin", "w_z", "w_xbc", "w_dt", "w_gate", "b_gate", "sgu_ln_g",
                             "sgu_ln_b", "sgu_ws", "sgu_bias", "pool_w", "pool_scale", "conv_w", "conv_b",
                             "dt_bias", "w_branch_a", "w_branch_b")]
    return pl.pallas_call(
        functools.partial(_mixin_body, seq_len=s),
        grid=(b, nt),
        in_specs=[tile(D),
                  pl.BlockSpec((1, HALO, D), lambda i, j: (i, jnp.maximum(j * hb - 1, 0), 0)),
                  pl.BlockSpec((1, HALO, D), lambda i, j: (i, jnp.minimum((j + 1) * hb, nt * hb - 1), 0))]
                 + [_layer_spec(c, l) for c in consts],
        out_specs=[tile(D), tile(D), tile(SSD_INNER), tile(CONV_DIM), tile(LANES)],
        out_shape=out_shapes,
        scratch_shapes=[pltpu.VMEM((tm + 2 * HALO, POOL_W), F32),
                        pltpu.VMEM((tm + 2 * HALO, CONV_DIM), F32),
                        pltpu.VMEM((tm, SGU_W), BF16),
                        pltpu.VMEM((tm, POOL_W), BF16)],
        compiler_params=_cparams(2),
        name="mixin",
    )(x, x, x, *consts)


def _pair_masks():
    lane = lax.broadcasted_iota(jnp.int32, (CHUNK, LANES), 1)
    return (jnp.where(lane < HEADDIM, 1.0, 0.0).astype(BF16),
            jnp.where(lane >= HEADDIM, 1.0, 0.0).astype(BF16))


def _pair_blockdiag(v, mlo, mhi):
    return jnp.concatenate([v * mlo, v * mhi], axis=0)


def _state_body(xcf_ref, dtf_ref, xcb_ref, dtb_ref, alog_ref, trit_ref, of_ref, ob_ref, h_ref):
    t = pl.program_id(1)
    nc = xcf_ref.shape[1] // CHUNK

    @pl.when(t == 0)
    def _():
        h_ref[...] = jnp.zeros_like(h_ref)

    mlo, mhi = _pair_masks()
    low_half = lax.broadcasted_iota(jnp.int32, (1, LANES), 1) < HEADDIM
    sts, cds = [], []
    for d, (xc_ref, dt_ref) in enumerate(((xcf_ref, dtf_ref), (xcb_ref, dtb_ref))):
        a_col = -jnp.exp(alog_ref[d])
        trit = trit_ref[d]
        sts.append([])
        cds.append([])
        for ci in range(nc):
            rs = slice(ci * CHUNK, (ci + 1) * CHUNK)
            dtt = dt_ref[0, rs, :].T[d * HEADS:(d + 1) * HEADS, :]
            dat = dtt * a_col
            hi, mid, lo = _split3(dat)
            r = _dot(jnp.concatenate([hi, mid, lo], axis=0), trit)
            rest = r[0:HEADS] + r[HEADS:2 * HEADS] + r[2 * HEADS:]
            wt = dtt * jnp.exp(rest)
            cds[d].append(jnp.exp(jnp.sum(dat, axis=1, keepdims=True)))
            bmt = xc_ref[0, rs, SSD_INNER:SSD_INNER + GN].astype(F32).T
            per_pair = []
            for j in range(HEADS // 2):
                g = (2 * j) // (HEADS // GROUPS)
                bg = bmt[g * STATE:(g + 1) * STATE, :]
                lhs = jnp.concatenate([bg * wt[2 * j:2 * j + 1, :], bg * wt[2 * j + 1:2 * j + 2, :]],
                                      axis=1).astype(BF16)
                rhs = _pair_blockdiag(xc_ref[0, rs, j * LANES:(j + 1) * LANES], mlo, mhi)
                per_pair.append(_dot(lhs, rhs))
            sts[d].append(per_pair)

    for d, o_ref in enumerate((of_ref, ob_ref)):
        order = range(nc) if d == 0 else range(nc - 1, -1, -1)
        for j in range(HEADS // 2):
            h = h_ref[d, j]
            for ci in order:
                o_ref[0, ci, j] = h.astype(BF16)
                cd = cds[d][ci]
                h = h * jnp.where(low_half, cd[2 * j:2 * j + 1, :], cd[2 * j + 1:2 * j + 2, :]) + sts[d][ci][j]
            h_ref[d, j] = h


def _state(xc, dt, p, l):
    b, s, _ = xc.shape
    ts = TS_STATE
    nt = s // ts
    nc = ts // CHUNK
    npair = HEADS // 2
    fwd = lambda w: pl.BlockSpec((1, ts, w), lambda i, j: (i, j, 0))
    bwd = lambda w: pl.BlockSpec((1, ts, w), lambda i, j: (i, nt - 1 - j, 0))
    oshape = jax.ShapeDtypeStruct((b, s // CHUNK, npair, STATE, LANES), BF16)
    return pl.pallas_call(
        _state_body,
        grid=(b, nt),
        in_specs=[fwd(CONV_DIM), fwd(LANES), bwd(CONV_DIM), bwd(LANES),
                  _layer_spec(p["alog_heads"], l), _const_spec(p["tri_strict_t"].shape)],
        out_specs=[pl.BlockSpec((1, nc, npair, STATE, LANES), lambda i, j: (i, j, 0, 0, 0)),
                   pl.BlockSpec((1, nc, npair, STATE, LANES), lambda i, j: (i, nt - 1 - j, 0, 0, 0))],
        out_shape=[oshape, oshape],
        scratch_shapes=[pltpu.VMEM((2, npair, STATE, LANES), F32)],
        compiler_params=_cparams(2),
        name="ssd_state",
    )(xc, dt, xc, dt, p["alog_heads"], p["tri_strict_t"])


def _column(mat, h):
    return jnp.broadcast_to(mat[:, h:h + 1], mat.shape)


def _mix_body(x_ref, ab_ref, g2_ref, z_ref, xc_ref, dt_ref, pf_ref, pb_ref, alog_ref, tri_ref,
              dskip_ref, normg_ref, wc_ref, wout_ref, o_ref, ybuf_ref, cin_ref):
    ts = x_ref.shape[1]
    nc = ts // CHUNK
    a_row = -jnp.exp(alog_ref[0:1, :]) * alog_ref[1:2, :]
    tri = tri_ref[...]
    lane = lax.broadcasted_iota(jnp.int32, (CHUNK, LANES), 1)
    rowi = lax.broadcasted_iota(jnp.int32, (CHUNK, LANES), 0)
    below = lane < rowi
    above = lane > rowi
    low_half = lane < HEADDIM
    mlo, mhi = _pair_masks()
    half = SSD_INNER // GROUPS

    def chunk(c, carry):
        r0 = pl.multiple_of(c * CHUNK, CHUNK)
        dt = dt_ref[0, pl.ds(r0, CHUNK), :]
        hi, mid, lo = _split3(dt * a_row)
        r = _dot(tri, jnp.concatenate([hi, mid, lo], axis=1))
        cum = r[:, :LANES] + r[:, LANES:2 * LANES] + r[:, 2 * LANES:]
        pcol = jnp.where(lane < HEADS, cum[:CHUNK], cum[CHUNK:]) * LOG2E
        prow = pcol.T
        dtrow = dt.T
        rsub = prow[0:2 * HEADS] - jnp.log2(dtrow[0:2 * HEADS])
        rdiag = jnp.log2(dtrow[0:HEADS] + dtrow[HEADS:2 * HEADS])

        cmf = xc_ref[0, pl.ds(r0, CHUNK), SSD_INNER + GN:].astype(F32)
        bm = xc_ref[0, pl.ds(r0, CHUNK), SSD_INNER:SSD_INNER + GN]
        cmr = pltpu.roll(cmf, STATE, axis=1)
        cbs, cdup = [], []
        for g in range(GROUPS):
            cmg = jnp.where(low_half if g == 0 else ~low_half, cmf, 0.0).astype(BF16)
            cbs.append(lax.dot_general(cmg, bm, (((1,), (1,)), ((), ())), preferred_element_type=F32))
            cdup.append(jnp.where(low_half, cmf, cmr) if g == 0 else jnp.where(low_half, cmr, cmf))

        for j in range(HEADS // 2):
            g = (2 * j) // (HEADS // GROUPS)
            ms, ces = [], []
            for h in (2 * j, 2 * j + 1):
                colf = _column(pcol, h)
                colb = _column(pcol, HEADS + h)
                ex = jnp.where(below, colf - rsub[h:h + 1, :],
                               jnp.where(above, colb - rsub[HEADS + h:HEADS + h + 1, :], rdiag[h:h + 1, :]))
                ms.append((cbs[g] * jnp.exp2(ex)).astype(BF16))
                ces.append((cdup[g] * jnp.exp2(jnp.where(low_half, colf, colb))).astype(BF16))
            xp = xc_ref[0, pl.ds(r0, CHUNK), j * LANES:(j + 1) * LANES]
            st = jnp.concatenate([pf_ref[0, c, j], pb_ref[0, c, j]], axis=0)
            lhs = jnp.concatenate(ms + ces, axis=1)
            rhs = jnp.concatenate([_pair_blockdiag(xp, mlo, mhi), _pair_blockdiag(st, mlo, mhi)], axis=0)
            y = _dot(lhs, rhs) + xp.astype(F32) * dskip_ref[:, j * LANES:(j + 1) * LANES]
            ybuf_ref[pl.ds(r0, CHUNK), j * LANES:(j + 1) * LANES] = y
        return carry

    lax.fori_loop(0, nc, chunk, 0, unroll=2)

    for ci in range(nc):
        rs = slice(ci * CHUNK, (ci + 1) * CHUNK)
        z = z_ref[0, rs, :].astype(F32)
        y = ybuf_ref[rs, :] * (z * _sigmoid(z))
        for g in range(GROUPS):
            cs = slice(g * half, (g + 1) * half)
            cin_ref[rs, cs] = _rms(y[:, cs], normg_ref[:, cs]).astype(BF16)

    cproj = _dot(cin_ref[...], wc_ref[...])
    merged = ab_ref[0].astype(F32) + g2_ref[0].astype(F32) * cproj
    o_ref[0] = x_ref[0] + _dot(merged.astype(BF16), wout_ref[...])


def _mix(x, ab, g2, z, xc, dt, prev_f, prev_b, p, l):
    b, s, _ = x.shape
    ts = TS_MIX
    nc = ts // CHUNK
    npair = HEADS // 2
    tile = lambda w: pl.BlockSpec((1, ts, w), lambda i, j: (i, j, 0))
    prev = pl.BlockSpec((1, nc, npair, STATE, LANES), lambda i, j: (i, j, 0, 0, 0))
    consts = [p["alog_all"], p["tri_incl"], p["d_skip"], p["ssd_norm"], p["w_branch_c"], p["w_out"]]
    return pl.pallas_call(
        _mix_body,
        grid=(b, s // ts),
        in_specs=[tile(D), tile(D), tile(D), tile(SSD_INNER), tile(CONV_DIM), tile(LANES), prev, prev]
                 + [_const_spec(c.shape) if c is p["tri_incl"] else _layer_spec(c, l) for c in consts],
        out_specs=tile(D),
        out_shape=jax.ShapeDtypeStruct((b, s, D), F32),
        scratch_shapes=[pltpu.VMEM((ts, SSD_INNER), F32), pltpu.VMEM((ts, SSD_INNER), BF16)],
        compiler_params=_cparams(2),
        name="ssd_mix",
    )(x, ab, g2, z, xc, dt, prev_f, prev_b, *consts)


def _kv_body(m_ref, g_ref, wkv_ref, k_ref, v_ref):
    mn = _rms(m_ref[0], g_ref[...]).astype(BF16)
    kv = _dot(mn, wkv_ref[...])
    k_ref[0] = kv[:, :D].astype(BF16)
    v_ref[0] = kv[:, D:].astype(BF16)


def _kv(mem, p, l):
    b, m, _ = mem.shape
    blk = pl.BlockSpec((1, m, D), lambda i: (i, 0, 0))
    return pl.pallas_call(
        _kv_body,
        grid=(b,),
        in_specs=[blk, _layer_spec(p["mem_norm"], l), _layer_spec(p["xattn_wkv"], l)],
        out_specs=[blk, blk],
        out_shape=[jax.ShapeDtypeStruct((b, m, D), BF16)] * 2,
        compiler_params=_cparams(1),
        name="xattn_kv",
    )(mem, p["mem_norm"], p["xattn_wkv"])


def _attn_body(x_ref, k_ref, v_ref, g_ref, wq_ref, wo_ref, o_ref, obuf_ref):
    x = x_ref[0]
    xn = _rms(x, g_ref[...]).astype(BF16)
    q = (_dot(xn, wq_ref[...]) * (XA_HD ** -0.5)).astype(BF16)
    for h in range(XA_HEADS):
        cs = slice(h * XA_HD, (h + 1) * XA_HD)
        s = lax.dot_general(q[:, cs], k_ref[0, :, cs], (((1,), (1,)), ((), ())),
                            preferred_element_type=F32)
        e = jnp.exp(s - jnp.max(s, axis=-1, keepdims=True))
        pr = (e * (1.0 / jnp.sum(e, axis=-1, keepdims=True))).astype(BF16)
        obuf_ref[:, cs] = _dot(pr, v_ref[0, :, cs]).astype(BF16)
    o_ref[0] = x + _dot(obuf_ref[...], wo_ref[...])


def _attn(x, k, v, p, l):
    b, s, _ = x.shape
    m = k.shape[1]
    tm = TM_ATTN
    tile = pl.BlockSpec((1, tm, D), lambda i, j: (i, j, 0))
    kvb = pl.BlockSpec((1, m, D), lambda i, j: (i, 0, 0))
    return pl.pallas_call(
        _attn_body,
        grid=(b, s // tm),
        in_specs=[tile, kvb, kvb, _layer_spec(p["xattn_norm"], l), _layer_spec(p["xattn_wq"], l),
                  _layer_spec(p["xattn_wo"], l)],
        out_specs=tile,
        out_shape=jax.ShapeDtypeStruct((b, s, D), F32),
        scratch_shapes=[pltpu.VMEM((tm, D), BF16)],
        compiler_params=_cparams(2),
        name="xattn",
    )(x, k, v, p["xattn_norm"], p["xattn_wq"], p["xattn_wo"])


def _constants():
    q = np.arange(CHUNK)
    lower_incl = (q[None, :] <= q[:, None]).astype(np.float32)
    upper_incl = (q[None, :] >= q[:, None]).astype(np.float32)
    strict_up = (q[None, :] > q[:, None]).astype(np.float32)
    strict_lo = (q[None, :] < q[:, None]).astype(np.float32)
    return dict(
        tri_incl=jnp.asarray(np.concatenate([lower_incl, upper_incl], axis=0), BF16),
        tri_strict_t=jnp.asarray(np.stack([strict_up.T, strict_lo.T]), BF16),
    )


def _prepare(raw):
    L = raw["w_in"].shape[0]
    w_in = raw["w_in"]
    o_pin = 2 * SGU_W
    o_z = o_pin + POOL_W
    o_xbc = o_z + SSD_INNER
    o_dt = o_xbc + CONV_DIM
    row = lambda a: a.reshape(L, 1, -1).astype(F32)
    p = dict(_constants())
    for k in ("ffn1_norm", "ffn2_norm", "mix_norm", "xattn_norm", "mem_norm", "b_gate", "sgu_ln_g", "sgu_ln_b",
              "pool_scale", "conv_b", "ssd_norm"):
        p[k] = row(raw[k])
    p["final_norm"] = raw["final_norm"].reshape(1, D).astype(F32)
    for k in ("ffn1_w13", "ffn1_w2", "ffn2_w13", "ffn2_w2", "w_gate", "sgu_ws", "pool_w", "w_branch_a",
              "w_branch_b", "w_branch_c", "w_out", "xattn_wq", "xattn_wkv", "xattn_wo"):
        p[k] = raw[k].astype(BF16)
    p["w_uv"] = w_in[:, :, :o_pin].astype(BF16)
    p["w_pin"] = w_in[:, :, o_pin:o_z].astype(BF16)
    p["w_z"] = w_in[:, :, o_z:o_xbc].astype(BF16)
    p["w_xbc"] = w_in[:, :, o_xbc:o_dt].astype(BF16)
    p["w_dt"] = jnp.pad(w_in[:, :, o_dt:], ((0, 0), (0, 0), (0, LANES - 2 * HEADS))).astype(BF16)
    p["dt_bias"] = jnp.pad(raw["dt_bias"].reshape(L, 1, 2 * HEADS), ((0, 0), (0, 0), (0, LANES - 2 * HEADS))).astype(F32)
    p["conv_w"] = raw["conv_w"].astype(F32)
    p["sgu_bias"] = jnp.broadcast_to(raw["sgu_bias"][:, :, :, None], (L, SGU_G, CHUNK, LANES)).astype(F32)
    alog = raw["a_log"].astype(F32)
    zeros = jnp.zeros((L, HEADS), F32)
    ones = jnp.ones((L, HEADS), F32)
    padw = LANES - 2 * HEADS
    def rows(vals, mask):
        v = jnp.pad(jnp.concatenate(vals, axis=-1), ((0, 0), (0, padw)))
        m = jnp.pad(jnp.concatenate(mask, axis=-1), ((0, 0), (0, padw)))
        return jnp.stack([v, m], axis=1)
    p["alog_all"] = rows([alog[:, 0], alog[:, 1]], [ones, ones])
    p["alog_heads"] = jnp.broadcast_to(alog[:, :, :, None], (L, 2, HEADS, LANES))
    p["d_skip"] = jnp.repeat(raw["d_skip"].astype(F32), HEADDIM, axis=-1).reshape(L, 1, SSD_INNER)
    return p, L


def _trunk(x, mem, p, depth):
    b, s, _ = x.shape
    for l in range(depth):
        xf = _ffn(x.reshape(b * s, D), p["ffn1_norm"], p["ffn1_w13"], p["ffn1_w2"],
                  p["final_norm"], False, l)
        x = xf.reshape(b, s, D)
        ab, g2, z, xc, dt = _mixin(x, p, l)
        prev_f, prev_b = _state(xc, dt, p, l)
        x = _mix(x, ab, g2, z, xc, dt, prev_f, prev_b, p, l)
        k, v = _kv(mem, p, l)
        x = _attn(x, k, v, p, l)
        xf = _ffn(x.reshape(b * s, D), p["ffn2_norm"], p["ffn2_w13"], p["ffn2_w2"],
                  p["final_norm"], l == depth - 1, l)
        x = xf.reshape(b, s, D)
    return x


def kernel(x_prompt, x_sample, mem_prompt, mem_sample, ffn1_norm, ffn1_w13, ffn1_w2, mix_norm, w_in, w_gate, b_gate, sgu_ln_g, sgu_ln_b, sgu_ws, sgu_bias, pool_w, pool_scale, conv_w, conv_b, dt_bias, a_log, d_skip, ssd_norm, w_branch_a, w_branch_b, w_branch_c, w_out, xattn_norm, mem_norm, xattn_wq, xattn_wkv, xattn_wo, ffn2_norm, ffn2_w13, ffn2_w2, final_norm):
    raw = dict(ffn1_norm=ffn1_norm, ffn1_w13=ffn1_w13, ffn1_w2=ffn1_w2, mix_norm=mix_norm, w_in=w_in,
               w_gate=w_gate, b_gate=b_gate, sgu_ln_g=sgu_ln_g, sgu_ln_b=sgu_ln_b, sgu_ws=sgu_ws,
               sgu_bias=sgu_bias, pool_w=pool_w, pool_scale=pool_scale, conv_w=conv_w, conv_b=conv_b,
               dt_bias=dt_bias, a_log=a_log, d_skip=d_skip, ssd_norm=ssd_norm, w_branch_a=w_branch_a,
               w_branch_b=w_branch_b, w_branch_c=w_branch_c, w_out=w_out, xattn_norm=xattn_norm,
               mem_norm=mem_norm, xattn_wq=xattn_wq, xattn_wkv=xattn_wkv, xattn_wo=xattn_wo,
               ffn2_norm=ffn2_norm, ffn2_w13=ffn2_w13, ffn2_w2=ffn2_w2, final_norm=final_norm)
    p, depth = _prepare(raw)
    return (_trunk(x_prompt, mem_prompt, p, depth), _trunk(x_sample, mem_sample, p, depth))
```

```python
import functools

import numpy as np
import jax
import jax.numpy as jnp
from jax import lax
from jax.experimental import pallas as pl
from jax.experimental.pallas import tpu as pltpu

F32 = jnp.float32
BF16 = jnp.bfloat16

D = 1024
EPS = 1e-6
D_FF = 2816
SGU_W = 512
SGU_G = 4
CHUNK = 128
POOL_W = 512
POOL_WINDOWS = (2, 4, 8, 16)
SSD_INNER = 1024
HEADDIM = 64
HEADS = 16
GROUPS = 2
STATE = 64
GN = GROUPS * STATE
CONV_DIM = SSD_INNER + 2 * GN
CONV_K = 4
CONV_LEFT = 2
XA_HEADS = 4
XA_HD = D // XA_HEADS
LANES = 128
LOG2E = 1.4426950408889634
HALO = 16

TM_FFN = 1024
TM_MIXIN = 512
TS_STATE = 1024
TS_MIX = 1024
TM_ATTN = 1024
FFN_FC = 256

V7X_VMEM_BYTES = 64 * 1024 * 1024
VMEM_LIMIT = V7X_VMEM_BYTES - 8 * 1024 * 1024


def _cparams(n_axes):
    return pltpu.CompilerParams(dimension_semantics=("arbitrary",) * n_axes,
                                vmem_limit_bytes=VMEM_LIMIT)


def _const_spec(shape):
    nd = len(shape)
    return pl.BlockSpec(shape, lambda *_: (0,) * nd, pipeline_mode=pl.Buffered(1))


def _layer_spec(stacked, l):
    nd = stacked.ndim - 1
    return pl.BlockSpec((None,) + stacked.shape[1:], lambda *_: (l,) + (0,) * nd,
                        pipeline_mode=pl.Buffered(1))


def _rms(x, g):
    return x * lax.rsqrt(jnp.mean(x * x, axis=-1, keepdims=True) + EPS) * g


def _sigmoid(x):
    return 1.0 / (1.0 + jnp.exp2(x * (-LOG2E)))


def _split3(x):
    hi = x.astype(BF16)
    r1 = x - hi.astype(F32)
    mid = r1.astype(BF16)
    lo = (r1 - mid.astype(F32)).astype(BF16)
    return hi, mid, lo


def _dot(a, b):
    return jnp.dot(a, b, preferred_element_type=F32)


def _ffn_body(x_ref, g_ref, w13_ref, w2_ref, fin_ref, o_ref, gbuf_ref, *, final):
    x = x_ref[...]
    xn = _rms(x, g_ref[...]).astype(BF16)
    for c in range(D_FF // FFN_FC):
        a = _dot(xn, w13_ref[:, c * FFN_FC:(c + 1) * FFN_FC])
        b = _dot(xn, w13_ref[:, D_FF + c * FFN_FC:D_FF + (c + 1) * FFN_FC])
        gbuf_ref[:, c * FFN_FC:(c + 1) * FFN_FC] = (a * _sigmoid(a) * b).astype(BF16)
    y = x + 0.5 * _dot(gbuf_ref[...], w2_ref[...])
    if final:
        y = _rms(y, fin_ref[...])
    o_ref[...] = y


def _ffn(x, g, w13, w2, fin, final, l):
    t = x.shape[0]
    tm = TM_FFN
    return pl.pallas_call(
        functools.partial(_ffn_body, final=final),
        grid=(t // tm,),
        in_specs=[pl.BlockSpec((tm, D), lambda i: (i, 0)),
                  _layer_spec(g, l), _layer_spec(w13, l), _layer_spec(w2, l), _const_spec((1, D))],
        out_specs=pl.BlockSpec((tm, D), lambda i: (i, 0)),
        out_shape=jax.ShapeDtypeStruct((t, D), F32),
        scratch_shapes=[pltpu.VMEM((tm, D_FF), BF16)],
        compiler_params=_cparams(1),
        name="ffn_final" if final else "ffn",
    )(x, g, w13, w2, fin)


def _mixin_body(x_ref, xp_ref, xn_ref, g_ref, wuv_ref, wpin_ref, wz_ref, wxbc_ref, wdt_ref,
                wgate_ref, bgate_ref, lng_ref, lnb_ref, ws_ref, sbias_ref, poolw_ref, pscale_ref,
                convw_ref, convb_ref, dtb_ref, wa_ref, wb_ref,
                ab_ref, g2_ref, z_ref, xc_ref, dt_ref,
                pin_ext, xbc_ext, a_buf, b_buf, uv_buf, graw_buf, conv_out, pool_out, *, seq_len):
    tm = x_ref.shape[1]
    t = pl.program_id(1)
    nt = pl.num_programs(1)
    g = g_ref[...]
    xp = jnp.where(t > 0, xp_ref[0], 0.0)
    xn = jnp.where(t < nt - 1, xn_ref[0], 0.0)
    n_main = _rms(x_ref[0], g).astype(BF16)
    n_prev = _rms(xp, g).astype(BF16)
    n_next = _rms(xn, g).astype(BF16)

    n_ext = jnp.concatenate([n_prev, n_main, n_next], axis=0)
    pin = _dot(n_ext, wpin_ref[...])
    for c in range(POOL_W // LANES):
        pin_ext[c] = pin[:, c * LANES:(c + 1) * LANES]
    xbc = _dot(n_ext, wxbc_ref[...])
    for c in range(CONV_DIM // LANES):
        xbc_ext[c] = xbc[:, c * LANES:(c + 1) * LANES]

    hm = tm // 2

    for c in range(CONV_DIM // LANES):
        cs = slice(c * LANES, (c + 1) * LANES)
        for e in range(2):
            acc = convb_ref[:, cs] + xbc_ext[c, pl.ds(HALO - CONV_LEFT + e, hm, stride=2), :] * convw_ref[0:1, cs]
            for k in range(1, CONV_K):
                acc = acc + xbc_ext[c, pl.ds(HALO - CONV_LEFT + k + e, hm, stride=2), :] * convw_ref[k:k + 1, cs]
            conv_out[c, pl.ds(e, hm, stride=2), :] = acc * _sigmoid(acc)
        xc_ref[0, :, cs] = conv_out[c].astype(BF16)

    raw = _dot(n_main, wdt_ref[...]) + dtb_ref[...]
    sp = jnp.maximum(raw, 0.0) + jnp.log(1.0 + jnp.exp(-jnp.abs(raw)))
    lane = lax.broadcasted_iota(jnp.int32, (tm, LANES), 1)
    dt_ref[0] = jnp.where(lane < 2 * HEADS, sp, 0.0)

    z_ref[0] = _dot(n_main, wz_ref[...]).astype(BF16)

    pos0 = t * tm + 2 * lax.broadcasted_iota(jnp.int32, (hm, LANES), 0)
    for gi, w in enumerate(POOL_WINDOWS):
        cs = slice(gi * LANES, (gi + 1) * LANES)
        for e in range(2):
            s = pin_ext[gi, pl.ds(HALO - w // 2 + e, hm, stride=2), :]
            for k in range(1, w):
                s = s + pin_ext[gi, pl.ds(HALO - w // 2 + k + e, hm, stride=2), :]
            pos = pos0 + e
            lo = jnp.clip(pos - w // 2, 0, seq_len)
            hi = jnp.clip(pos + w - w // 2, 0, seq_len)
            cnt = (hi - lo).astype(F32)
            pool_out[gi, pl.ds(e, hm, stride=2), :] = s / cnt - pin_ext[gi, pl.ds(HALO + e, hm, stride=2), :]
        b_buf[:, cs] = (_dot(pool_out[gi].astype(BF16), poolw_ref[gi]) * pscale_ref[:, cs]).astype(BF16)

    uv_buf[...] = _dot(n_main, wuv_ref[...])

    def gelu(v):
        return v * _sigmoid(v * (1.5957691216057308 + 0.07135481627260025 * (v * v)))

    vns = []
    for c in range(tm // CHUNK):
        v = gelu(uv_buf[c * CHUNK:(c + 1) * CHUNK, SGU_W:])
        vc = v - jnp.mean(v, axis=-1, keepdims=True)
        var = jnp.mean(vc * vc, axis=-1, keepdims=True)
        vns.append((vc * lax.rsqrt(var + EPS) * lng_ref[...] + lnb_ref[...]).astype(BF16))
    for gi in range(SGU_G):
        cs = slice(gi * LANES, (gi + 1) * LANES)
        mixed = _dot(ws_ref[gi], jnp.concatenate([vn[:, cs] for vn in vns], axis=1))
        for c in range(tm // CHUNK):
            rs = slice(c * CHUNK, (c + 1) * CHUNK)
            a_buf[rs, cs] = (gelu(uv_buf[rs, cs]) * (mixed[:, c * LANES:(c + 1) * LANES] + sbias_ref[gi])).astype(BF16)

    graw_buf[...] = _dot(n_main, wgate_ref[...])
    gates = _sigmoid(graw_buf[...] + bgate_ref[...])
    ab = gates[:, :D] * _dot(a_buf[...], wa_ref[...]) + gates[:, D:2 * D] * _dot(b_buf[...], wb_ref[...])
    ab_ref[0] = ab.astype(BF16)
    g2_ref[0] = gates[:, 2 * D:].astype(BF16)


def _mixin(x, p, l):
    b, s, _ = x.shape
    tm = TM_MIXIN
    nt = s // tm
    hb = tm // HALO
    tile = lambda w: pl.BlockSpec((1, tm, w), lambda i, j: (i, j, 0))
    out_shapes = [jax.ShapeDtypeStruct((b, s, D), BF16),
                  jax.ShapeDtypeStruct((b, s, D), BF16),
                  jax.ShapeDtypeStruct((b, s, SSD_INNER), BF16),
                  jax.ShapeDtypeStruct((b, s, CONV_DIM), BF16),
                  jax.ShapeDtypeStruct((b, s, LANES), F32)]
    consts = [p[k] for k in ("mix_norm", "w_uv", "w_pin", "w_z", "w_xbc", "w_dt", "w_gate", "b_gate", "sgu_ln_g",
                             "sgu_ln_b", "sgu_ws", "sgu_bias", "pool_w", "pool_scale", "conv_w", "conv_b",
                             "dt_bias", "w_branch_a", "w_branch_b")]
    return pl.pallas_call(
        functools.partial(_mixin_body, seq_len=s),
        grid=(b, nt),
        in_specs=[tile(D),
                  pl.BlockSpec((1, HALO, D), lambda i, j: (i, jnp.maximum(j * hb - 1, 0), 0)),
                  pl.BlockSpec((1, HALO, D), lambda i, j: (i, jnp.minimum((j + 1) * hb, nt * hb - 1), 0))]
                 + [_layer_spec(c, l) for c in consts],
        out_specs=[tile(D), tile(D), tile(SSD_INNER), tile(CONV_DIM), tile(LANES)],
        out_shape=out_shapes,
        scratch_shapes=[pltpu.VMEM((POOL_W // LANES, tm + 2 * HALO, LANES), F32),
                        pltpu.VMEM((CONV_DIM // LANES, tm + 2 * HALO, LANES), F32),
                        pltpu.VMEM((tm, SGU_W), BF16),
                        pltpu.VMEM((tm, POOL_W), BF16),
                        pltpu.VMEM((tm, 2 * SGU_W), F32),
                        pltpu.VMEM((tm, 3 * D), F32),
                        pltpu.VMEM((CONV_DIM // LANES, tm, LANES), F32),
                        pltpu.VMEM((POOL_W // LANES, tm, LANES), F32)],
        compiler_params=_cparams(2),
        name="mixin",
    )(x, x, x, *consts)


def _pair_masks():
    lane = lax.broadcasted_iota(jnp.int32, (CHUNK // 2, LANES), 1)
    ones = jnp.uint32(0xFFFFFFFF)
    zero = jnp.uint32(0)
    return jnp.where(lane < HEADDIM, ones, zero), jnp.where(lane >= HEADDIM, ones, zero)


def _pair_blockdiag(v, mlo, mhi):
    bits = pltpu.bitcast(v, jnp.uint32)
    return jnp.concatenate([pltpu.bitcast(bits & mlo, BF16), pltpu.bitcast(bits & mhi, BF16)], axis=0)


def _state_body(xcf_ref, dtf_ref, xcb_ref, dtb_ref, alog_ref, trit_ref, of_ref, ob_ref, h_ref):
    t = pl.program_id(1)
    nc = xcf_ref.shape[1] // CHUNK

    @pl.when(t == 0)
    def _():
        h_ref[...] = jnp.zeros_like(h_ref)

    mlo, mhi = _pair_masks()
    low_half = lax.broadcasted_iota(jnp.int32, (1, LANES), 1) < HEADDIM
    sts, cds = [], []
    for d, (xc_ref, dt_ref) in enumerate(((xcf_ref, dtf_ref), (xcb_ref, dtb_ref))):
        a_col = -jnp.exp(alog_ref[d])
        trit = trit_ref[d]
        sts.append([])
        cds.append([])
        for ci in range(nc):
            rs = slice(ci * CHUNK, (ci + 1) * CHUNK)
            dtt = dt_ref[0, rs, :].T[d * HEADS:(d + 1) * HEADS, :]
            dat = dtt * a_col
            hi, mid, lo = _split3(dat)
            r = _dot(jnp.concatenate([hi, mid, lo], axis=0), trit)
            rest = r[0:HEADS] + r[HEADS:2 * HEADS] + r[2 * HEADS:]
            wt = dtt * jnp.exp(rest)
            cds[d].append(jnp.exp(jnp.sum(dat, axis=1, keepdims=True)))
            bmt = xc_ref[0, rs, SSD_INNER:SSD_INNER + GN].astype(F32).T
            per_pair = []
            for j in range(HEADS // 2):
                g = (2 * j) // (HEADS // GROUPS)
                bg = bmt[g * STATE:(g + 1) * STATE, :]
                lhs = jnp.concatenate([bg * wt[2 * j:2 * j + 1, :], bg * wt[2 * j + 1:2 * j + 2, :]],
                                      axis=1).astype(BF16)
                rhs = _pair_blockdiag(xc_ref[0, rs, j * LANES:(j + 1) * LANES], mlo, mhi)
                per_pair.append(_dot(lhs, rhs))
            sts[d].append(per_pair)

    for d, o_ref in enumerate((of_ref, ob_ref)):
        order = range(nc) if d == 0 else range(nc - 1, -1, -1)
        for j in range(HEADS // 2):
            h = h_ref[d, j]
            for ci in order:
                o_ref[0, ci, j] = h.astype(BF16)
                cd = cds[d][ci]
                h = h * jnp.where(low_half, cd[2 * j:2 * j + 1, :], cd[2 * j + 1:2 * j + 2, :]) + sts[d][ci][j]
            h_ref[d, j] = h


def _state(xc, dt, p, l):
    b, s, _ = xc.shape
    ts = TS_STATE
    nt = s // ts
    nc = ts // CHUNK
    npair = HEADS // 2
    fwd = lambda w: pl.BlockSpec((1, ts, w), lambda i, j: (i, j, 0))
    bwd = lambda w: pl.BlockSpec((1, ts, w), lambda i, j: (i, nt - 1 - j, 0))
    oshape = jax.ShapeDtypeStruct((b, s // CHUNK, npair, STATE, LANES), BF16)
    return pl.pallas_call(
        _state_body,
        grid=(b, nt),
        in_specs=[fwd(CONV_DIM), fwd(LANES), bwd(CONV_DIM), bwd(LANES),
                  _layer_spec(p["alog_heads"], l), _const_spec(p["tri_strict_t"].shape)],
        out_specs=[pl.BlockSpec((1, nc, npair, STATE, LANES), lambda i, j: (i, j, 0, 0, 0)),
                   pl.BlockSpec((1, nc, npair, STATE, LANES), lambda i, j: (i, nt - 1 - j, 0, 0, 0))],
        out_shape=[oshape, oshape],
        scratch_shapes=[pltpu.VMEM((2, npair, STATE, LANES), F32)],
        compiler_params=_cparams(2),
        name="ssd_state",
    )(xc, dt, xc, dt, p["alog_heads"], p["tri_strict_t"])


def _column(mat, h):
    return jnp.broadcast_to(mat[:, h:h + 1], mat.shape)


def _mix_body(x_ref, ab_ref, g2_ref, z_ref, xc_ref, dt_ref, pf_ref, pb_ref, alog_ref, tri_ref,
              dskip_ref, normg_ref, wc_ref, wout_ref, o_ref, ybuf_ref, cin_ref):
    ts = x_ref.shape[1]
    nc = ts // CHUNK
    a_row = -jnp.exp(alog_ref[0:1, :]) * alog_ref[1:2, :]
    tri = tri_ref[...]
    lane = lax.broadcasted_iota(jnp.int32, (CHUNK, LANES), 1)
    rowi = lax.broadcasted_iota(jnp.int32, (CHUNK, LANES), 0)
    below = lane < rowi
    above = lane > rowi
    low_half = lane < HEADDIM
    mlo, mhi = _pair_masks()
    half = SSD_INNER // GROUPS

    def chunk(c, carry):
        r0 = pl.multiple_of(c * CHUNK, CHUNK)
        dt = dt_ref[0, pl.ds(r0, CHUNK), :]
        hi, mid, lo = _split3(dt * a_row)
        r = _dot(tri, jnp.concatenate([hi, mid, lo], axis=1))
        cum = r[:, :LANES] + r[:, LANES:2 * LANES] + r[:, 2 * LANES:]
        pcol = jnp.where(lane < HEADS, cum[:CHUNK], cum[CHUNK:]) * LOG2E
        prow = pcol.T
        dtrow = dt.T
        rsub = prow[0:2 * HEADS] - jnp.log2(dtrow[0:2 * HEADS])
        rdiag = jnp.log2(dtrow[0:HEADS] + dtrow[HEADS:2 * HEADS])

        cmf = xc_ref[0, pl.ds(r0, CHUNK), SSD_INNER + GN:].astype(F32)
        bm = xc_ref[0, pl.ds(r0, CHUNK), SSD_INNER:SSD_INNER + GN]
        cmr = pltpu.roll(cmf, STATE, axis=1)
        cmg = jnp.concatenate([jnp.where(low_half, cmf, 0.0), jnp.where(low_half, 0.0, cmf)], axis=0).astype(BF16)
        cb2 = lax.dot_general(cmg, bm, (((1,), (1,)), ((), ())), preferred_element_type=F32)
        cbs = [cb2[:CHUNK], cb2[CHUNK:]]
        cdup = [jnp.where(low_half, cmf, cmr), jnp.where(low_half, cmr, cmf)]

        for j in range(HEADS // 2):
            g = (2 * j) // (HEADS // GROUPS)
            ms, ces = [], []
            for h in (2 * j, 2 * j + 1):
                colf = _column(pcol, h)
                colb = _column(pcol, HEADS + h)
                ex = jnp.where(below, colf - rsub[h:h + 1, :],
                               jnp.where(above, colb - rsub[HEADS + h:HEADS + h + 1, :], rdiag[h:h + 1, :]))
                ms.append((cbs[g] * jnp.exp2(ex)).astype(BF16))
                ces.append((cdup[g] * jnp.exp2(jnp.where(low_half, colf, colb))).astype(BF16))
            xp = xc_ref[0, pl.ds(r0, CHUNK), j * LANES:(j + 1) * LANES]
            st = jnp.concatenate([pf_ref[0, c, j], pb_ref[0, c, j]], axis=0)
            lhs = jnp.concatenate(ms + ces, axis=1)
            rhs = jnp.concatenate([_pair_blockdiag(xp, mlo, mhi), _pair_blockdiag(st, mlo, mhi)], axis=0)
            y = _dot(lhs, rhs) + xp.astype(F32) * dskip_ref[:, j * LANES:(j + 1) * LANES]
            ybuf_ref[pl.ds(r0, CHUNK), j * LANES:(j + 1) * LANES] = y
        return carry

    lax.fori_loop(0, nc, chunk, 0, unroll=2)

    for ci in range(nc):
        rs = slice(ci * CHUNK, (ci + 1) * CHUNK)
        z = z_ref[0, rs, :].astype(F32)
        y = ybuf_ref[rs, :] * (z * _sigmoid(z))
        for g in range(GROUPS):
            cs = slice(g * half, (g + 1) * half)
            cin_ref[rs, cs] = _rms(y[:, cs], normg_ref[:, cs]).astype(BF16)

    cproj = _dot(cin_ref[...], wc_ref[...])
    merged = ab_ref[0].astype(F32) + g2_ref[0].astype(F32) * cproj
    o_ref[0] = x_ref[0] + _dot(merged.astype(BF16), wout_ref[...])


def _mix(x, ab, g2, z, xc, dt, prev_f, prev_b, p, l):
    b, s, _ = x.shape
    ts = TS_MIX
    nc = ts // CHUNK
    npair = HEADS // 2
    tile = lambda w: pl.BlockSpec((1, ts, w), lambda i, j: (i, j, 0))
    prev = pl.BlockSpec((1, nc, npair, STATE, LANES), lambda i, j: (i, j, 0, 0, 0))
    consts = [p["alog_all"], p["tri_incl"], p["d_skip"], p["ssd_norm"], p["w_branch_c"], p["w_out"]]
    return pl.pallas_call(
        _mix_body,
        grid=(b, s // ts),
        in_specs=[tile(D), tile(D), tile(D), tile(SSD_INNER), tile(CONV_DIM), tile(LANES), prev, prev]
                 + [_const_spec(c.shape) if c is p["tri_incl"] else _layer_spec(c, l) for c in consts],
        out_specs=tile(D),
        out_shape=jax.ShapeDtypeStruct((b, s, D), F32),
        scratch_shapes=[pltpu.VMEM((ts, SSD_INNER), F32), pltpu.VMEM((ts, SSD_INNER), BF16)],
        compiler_params=_cparams(2),
        name="ssd_mix",
    )(x, ab, g2, z, xc, dt, prev_f, prev_b, *consts)


def _kv_body(m_ref, g_ref, wkv_ref, k_ref, v_ref):
    mn = _rms(m_ref[0], g_ref[...]).astype(BF16)
    kv = _dot(mn, wkv_ref[...])
    k_ref[0] = kv[:, :D].astype(BF16)
    v_ref[0] = kv[:, D:].astype(BF16)


def _kv(mem, p, l):
    b, m, _ = mem.shape
    blk = pl.BlockSpec((1, m, D), lambda i: (i, 0, 0))
    return pl.pallas_call(
        _kv_body,
        grid=(b,),
        in_specs=[blk, _layer_spec(p["mem_norm"], l), _layer_spec(p["xattn_wkv"], l)],
        out_specs=[blk, blk],
        out_shape=[jax.ShapeDtypeStruct((b, m, D), BF16)] * 2,
        compiler_params=_cparams(1),
        name="xattn_kv",
    )(mem, p["mem_norm"], p["xattn_wkv"])


def _attn_body(x_ref, k_ref, v_ref, g_ref, wq_ref, wo_ref, o_ref, obuf_ref):
    x = x_ref[0]
    xn = _rms(x, g_ref[...]).astype(BF16)
    q = (_dot(xn, wq_ref[...]) * (XA_HD ** -0.5)).astype(BF16)
    for h in range(XA_HEADS):
        cs = slice(h * XA_HD, (h + 1) * XA_HD)
        s = lax.dot_general(q[:, cs], k_ref[0, :, cs], (((1,), (1,)), ((), ())),
                            preferred_element_type=F32)
        e = jnp.exp(s - jnp.max(s, axis=-1, keepdims=True))
        o = _dot(e.astype(BF16), v_ref[0, :, cs]) * (1.0 / jnp.sum(e, axis=-1, keepdims=True))
        obuf_ref[:, cs] = o.astype(BF16)
    o_ref[0] = x + _dot(obuf_ref[...], wo_ref[...])


def _attn(x, k, v, p, l):
    b, s, _ = x.shape
    m = k.shape[1]
    tm = TM_ATTN
    tile = pl.BlockSpec((1, tm, D), lambda i, j: (i, j, 0))
    kvb = pl.BlockSpec((1, m, D), lambda i, j: (i, 0, 0))
    return pl.pallas_call(
        _attn_body,
        grid=(b, s // tm),
        in_specs=[tile, kvb, kvb, _layer_spec(p["xattn_norm"], l), _layer_spec(p["xattn_wq"], l),
                  _layer_spec(p["xattn_wo"], l)],
        out_specs=tile,
        out_shape=jax.ShapeDtypeStruct((b, s, D), F32),
        scratch_shapes=[pltpu.VMEM((tm, D), BF16)],
        compiler_params=_cparams(2),
        name="xattn",
    )(x, k, v, p["xattn_norm"], p["xattn_wq"], p["xattn_wo"])


def _constants():
    q = np.arange(CHUNK)
    lower_incl = (q[None, :] <= q[:, None]).astype(np.float32)
    upper_incl = (q[None, :] >= q[:, None]).astype(np.float32)
    strict_up = (q[None, :] > q[:, None]).astype(np.float32)
    strict_lo = (q[None, :] < q[:, None]).astype(np.float32)
    return dict(
        tri_incl=jnp.asarray(np.concatenate([lower_incl, upper_incl], axis=0), BF16),
        tri_strict_t=jnp.asarray(np.stack([strict_up.T, strict_lo.T]), BF16),
    )


def _prepare(raw):
    L = raw["w_in"].shape[0]
    w_in = raw["w_in"]
    o_pin = 2 * SGU_W
    o_z = o_pin + POOL_W
    o_xbc = o_z + SSD_INNER
    o_dt = o_xbc + CONV_DIM
    row = lambda a: a.reshape(L, 1, -1).astype(F32)
    p = dict(_constants())
    for k in ("ffn1_norm", "ffn2_norm", "mix_norm", "xattn_norm", "mem_norm", "b_gate", "sgu_ln_g", "sgu_ln_b",
              "pool_scale", "conv_b", "ssd_norm"):
        p[k] = row(raw[k])
    p["final_norm"] = raw["final_norm"].reshape(1, D).astype(F32)
    for k in ("ffn1_w13", "ffn1_w2", "ffn2_w13", "ffn2_w2", "w_gate", "sgu_ws", "pool_w", "w_branch_a",
              "w_branch_b", "w_branch_c", "w_out", "xattn_wq", "xattn_wkv", "xattn_wo"):
        p[k] = raw[k].astype(BF16)
    p["w_uv"] = w_in[:, :, :o_pin].astype(BF16)
    p["w_pin"] = w_in[:, :, o_pin:o_z].astype(BF16)
    p["w_z"] = w_in[:, :, o_z:o_xbc].astype(BF16)
    p["w_xbc"] = w_in[:, :, o_xbc:o_dt].astype(BF16)
    p["w_dt"] = jnp.pad(w_in[:, :, o_dt:], ((0, 0), (0, 0), (0, LANES - 2 * HEADS))).astype(BF16)
    p["dt_bias"] = jnp.pad(raw["dt_bias"].reshape(L, 1, 2 * HEADS), ((0, 0), (0, 0), (0, LANES - 2 * HEADS))).astype(F32)
    p["conv_w"] = raw["conv_w"].astype(F32)
    p["sgu_bias"] = jnp.broadcast_to(raw["sgu_bias"][:, :, :, None], (L, SGU_G, CHUNK, LANES)).astype(F32)
    alog = raw["a_log"].astype(F32)
    lanes_used = jnp.pad(alog.reshape(L, 1, 2 * HEADS), ((0, 0), (0, 0), (0, LANES - 2 * HEADS)))
    lanes_mask = jnp.pad(jnp.ones((L, 1, 2 * HEADS), F32), ((0, 0), (0, 0), (0, LANES - 2 * HEADS)))
    p["alog_all"] = jnp.concatenate([lanes_used, lanes_mask], axis=1)
    p["alog_heads"] = jnp.broadcast_to(alog[:, :, :, None], (L, 2, HEADS, LANES))
    p["d_skip"] = jnp.repeat(raw["d_skip"].astype(F32), HEADDIM, axis=-1).reshape(L, 1, SSD_INNER)
    return p, L


def _trunk(x, mem, p, depth):
    b, s, _ = x.shape
    for l in range(depth):
        xf = _ffn(x.reshape(b * s, D), p["ffn1_norm"], p["ffn1_w13"], p["ffn1_w2"],
                  p["final_norm"], False, l)
        x = xf.reshape(b, s, D)
        ab, g2, z, xc, dt = _mixin(x, p, l)
        prev_f, prev_b = _state(xc, dt, p, l)
        x = _mix(x, ab, g2, z, xc, dt, prev_f, prev_b, p, l)
        k, v = _kv(mem, p, l)
        x = _attn(x, k, v, p, l)
        xf = _ffn(x.reshape(b * s, D), p["ffn2_norm"], p["ffn2_w13"], p["ffn2_w2"],
                  p["final_norm"], l == depth - 1, l)
        x = xf.reshape(b, s, D)
    return x


def kernel(x_prompt, x_sample, mem_prompt, mem_sample, ffn1_norm, ffn1_w13, ffn1_w2, mix_norm, w_in, w_gate, b_gate, sgu_ln_g, sgu_ln_b, sgu_ws, sgu_bias, pool_w, pool_scale, conv_w, conv_b, dt_bias, a_log, d_skip, ssd_norm, w_branch_a, w_branch_b, w_branch_c, w_out, xattn_norm, mem_norm, xattn_wq, xattn_wkv, xattn_wo, ffn2_norm, ffn2_w13, ffn2_w2, final_norm):
    raw = dict(ffn1_norm=ffn1_norm, ffn1_w13=ffn1_w13, ffn1_w2=ffn1_w2, mix_norm=mix_norm, w_in=w_in,
               w_gate=w_gate, b_gate=b_gate, sgu_ln_g=sgu_ln_g, sgu_ln_b=sgu_ln_b, sgu_ws=sgu_ws,
               sgu_bias=sgu_bias, pool_w=pool_w, pool_scale=pool_scale, conv_w=conv_w, conv_b=conv_b,
               dt_bias=dt_bias, a_log=a_log, d_skip=d_skip, ssd_norm=ssd_norm, w_branch_a=w_branch_a,
               w_branch_b=w_branch_b, w_branch_c=w_branch_c, w_out=w_out, xattn_norm=xattn_norm,
               mem_norm=mem_norm, xattn_wq=xattn_wq, xattn_wkv=xattn_wkv, xattn_wo=xattn_wo,
               ffn2_norm=ffn2_norm, ffn2_w13=ffn2_w13, ffn2_w2=ffn2_w2, final_norm=final_norm)
    p, depth = _prepare(raw)
    return (_trunk(x_prompt, mem_prompt, p, depth), _trunk(x_sample, mem_sample, p, depth))
```

```python
import functools

import numpy as np
import jax
import jax.numpy as jnp
from jax import lax
from jax.experimental import pallas as pl
from jax.experimental.pallas import tpu as pltpu

F32 = jnp.float32
BF16 = jnp.bfloat16

D = 1024
EPS = 1e-6
D_FF = 2816
SGU_W = 512
SGU_G = 4
CHUNK = 128
POOL_W = 512
POOL_WINDOWS = (2, 4, 8, 16)
SSD_INNER = 1024
HEADDIM = 64
HEADS = 16
GROUPS = 2
STATE = 64
GN = GROUPS * STATE
CONV_DIM = SSD_INNER + 2 * GN
CONV_K = 4
CONV_LEFT = 2
XA_HEADS = 4
XA_HD = D // XA_HEADS
LANES = 128
LOG2E = 1.4426950408889634
HALO = 16

TM_FFN = 1024
TM_MIXIN = 512
TS_STATE = 1024
TS_MIX = 1024
TM_ATTN = 1024
FFN_FC = 256

V7X_VMEM_BYTES = 64 * 1024 * 1024
VMEM_LIMIT = V7X_VMEM_BYTES - 8 * 1024 * 1024


def _cparams(n_axes):
    return pltpu.CompilerParams(dimension_semantics=("arbitrary",) * n_axes,
                                vmem_limit_bytes=VMEM_LIMIT)


def _const_spec(shape):
    nd = len(shape)
    return pl.BlockSpec(shape, lambda *_: (0,) * nd, pipeline_mode=pl.Buffered(1))


def _layer_spec(stacked, l):
    nd = stacked.ndim - 1
    return pl.BlockSpec((None,) + stacked.shape[1:], lambda *_: (l,) + (0,) * nd,
                        pipeline_mode=pl.Buffered(1))


def _rms(x, g):
    return x * lax.rsqrt(jnp.mean(x * x, axis=-1, keepdims=True) + EPS) * g


def _sigmoid(x):
    return 1.0 / (1.0 + jnp.exp2(x * (-LOG2E)))


def _split3(x):
    hi = x.astype(BF16)
    r1 = x - hi.astype(F32)
    mid = r1.astype(BF16)
    lo = (r1 - mid.astype(F32)).astype(BF16)
    return hi, mid, lo


def _dot(a, b):
    return jnp.dot(a, b, preferred_element_type=F32)


def _ffn_body(x_ref, g_ref, w13_ref, w2_ref, fin_ref, o_ref, gbuf_ref, *, final):
    x = x_ref[...]
    xn = _rms(x, g_ref[...]).astype(BF16)
    for c in range(D_FF // FFN_FC):
        a = _dot(xn, w13_ref[:, c * FFN_FC:(c + 1) * FFN_FC])
        b = _dot(xn, w13_ref[:, D_FF + c * FFN_FC:D_FF + (c + 1) * FFN_FC])
        gbuf_ref[:, c * FFN_FC:(c + 1) * FFN_FC] = (a * _sigmoid(a) * b).astype(BF16)
    y = x + 0.5 * _dot(gbuf_ref[...], w2_ref[...])
    if final:
        y = _rms(y, fin_ref[...])
    o_ref[...] = y


def _ffn(x, g, w13, w2, fin, final, l):
    t = x.shape[0]
    tm = TM_FFN
    return pl.pallas_call(
        functools.partial(_ffn_body, final=final),
        grid=(t // tm,),
        in_specs=[pl.BlockSpec((tm, D), lambda i: (i, 0)),
                  _layer_spec(g, l), _layer_spec(w13, l), _layer_spec(w2, l), _const_spec((1, D))],
        out_specs=pl.BlockSpec((tm, D), lambda i: (i, 0)),
        out_shape=jax.ShapeDtypeStruct((t, D), F32),
        scratch_shapes=[pltpu.VMEM((tm, D_FF), BF16)],
        compiler_params=_cparams(1),
        name="ffn_final" if final else "ffn",
    )(x, g, w13, w2, fin)


def _mixin_body(x_ref, xp_ref, xn_ref, g_ref, wuv_ref, wpin_ref, wz_ref, wxbc_ref, wdt_ref,
                wgate_ref, bgate_ref, lng_ref, lnb_ref, ws_ref, sbias_ref, poolw_ref, pscale_ref,
                convw_ref, convb_ref, dtb_ref, wa_ref, wb_ref,
                ab_ref, g2_ref, z_ref, xc_ref, dt_ref,
                pin_ext, xbc_ext, a_buf, b_buf, uv_buf, graw_buf, conv_out, pool_out, *, seq_len):
    tm = x_ref.shape[1]
    t = pl.program_id(1)
    nt = pl.num_programs(1)
    g = g_ref[...]
    xp = jnp.where(t > 0, xp_ref[0], 0.0)
    xn = jnp.where(t < nt - 1, xn_ref[0], 0.0)
    n_main = _rms(x_ref[0], g).astype(BF16)
    n_prev = _rms(xp, g).astype(BF16)
    n_next = _rms(xn, g).astype(BF16)

    n_ext = jnp.concatenate([n_prev, n_main, n_next], axis=0)
    pin = _dot(n_ext, wpin_ref[...])
    for c in range(POOL_W // LANES):
        pin_ext[c] = pin[:, c * LANES:(c + 1) * LANES]
    xbc = _dot(n_ext, wxbc_ref[...])
    for c in range(CONV_DIM // LANES):
        xbc_ext[c] = xbc[:, c * LANES:(c + 1) * LANES]
    uv_buf[...] = _dot(n_main, wuv_ref[...])
    graw_buf[...] = _dot(n_main, wgate_ref[...])
    z_ref[0] = _dot(n_main, wz_ref[...]).astype(BF16)
    raw = _dot(n_main, wdt_ref[...]) + dtb_ref[...]

    hm = tm // 2

    sp = jnp.maximum(raw, 0.0) + jnp.log(1.0 + jnp.exp(-jnp.abs(raw)))
    lane = lax.broadcasted_iota(jnp.int32, (tm, LANES), 1)
    dt_ref[0] = jnp.where(lane < 2 * HEADS, sp, 0.0)

    pos0 = t * tm + 2 * lax.broadcasted_iota(jnp.int32, (hm, LANES), 0)
    for gi, w in enumerate(POOL_WINDOWS):
        cs = slice(gi * LANES, (gi + 1) * LANES)
        for e in range(2):
            s = pin_ext[gi, pl.ds(HALO - w // 2 + e, hm, stride=2), :]
            for k in range(1, w):
                s = s + pin_ext[gi, pl.ds(HALO - w // 2 + k + e, hm, stride=2), :]
            pos = pos0 + e
            lo = jnp.clip(pos - w // 2, 0, seq_len)
            hi = jnp.clip(pos + w - w // 2, 0, seq_len)
            cnt = (hi - lo).astype(F32)
            pool_out[gi, pl.ds(e, hm, stride=2), :] = s / cnt - pin_ext[gi, pl.ds(HALO + e, hm, stride=2), :]
        b_buf[:, cs] = (_dot(pool_out[gi].astype(BF16), poolw_ref[gi]) * pscale_ref[:, cs]).astype(BF16)

    def gelu(v):
        return v * _sigmoid(v * (1.5957691216057308 + 0.07135481627260025 * (v * v)))

    vns = []
    for c in range(tm // CHUNK):
        v = gelu(uv_buf[c * CHUNK:(c + 1) * CHUNK, SGU_W:])
        vc = v - jnp.mean(v, axis=-1, keepdims=True)
        var = jnp.mean(vc * vc, axis=-1, keepdims=True)
        vns.append((vc * lax.rsqrt(var + EPS) * lng_ref[...] + lnb_ref[...]).astype(BF16))
    for gi in range(SGU_G):
        cs = slice(gi * LANES, (gi + 1) * LANES)
        mixed = _dot(ws_ref[gi], jnp.concatenate([vn[:, cs] for vn in vns], axis=1))
        for c in range(tm // CHUNK):
            rs = slice(c * CHUNK, (c + 1) * CHUNK)
            a_buf[rs, cs] = (gelu(uv_buf[rs, cs]) * (mixed[:, c * LANES:(c + 1) * LANES] + sbias_ref[gi])).astype(BF16)

    for c in range(CONV_DIM // LANES):
        cs = slice(c * LANES, (c + 1) * LANES)
        for e in range(2):
            acc = convb_ref[:, cs] + xbc_ext[c, pl.ds(HALO - CONV_LEFT + e, hm, stride=2), :] * convw_ref[0:1, cs]
            for k in range(1, CONV_K):
                acc = acc + xbc_ext[c, pl.ds(HALO - CONV_LEFT + k + e, hm, stride=2), :] * convw_ref[k:k + 1, cs]
            conv_out[c, pl.ds(e, hm, stride=2), :] = acc * _sigmoid(acc)
        xc_ref[0, :, cs] = conv_out[c].astype(BF16)

    gates = _sigmoid(graw_buf[...] + bgate_ref[...])
    ab = gates[:, :D] * _dot(a_buf[...], wa_ref[...]) + gates[:, D:2 * D] * _dot(b_buf[...], wb_ref[...])
    ab_ref[0] = ab.astype(BF16)
    g2_ref[0] = gates[:, 2 * D:].astype(BF16)


def _mixin(x, p, l):
    b, s, _ = x.shape
    tm = TM_MIXIN
    nt = s // tm
    hb = tm // HALO
    tile = lambda w: pl.BlockSpec((1, tm, w), lambda i, j: (i, j, 0))
    out_shapes = [jax.ShapeDtypeStruct((b, s, D), BF16),
                  jax.ShapeDtypeStruct((b, s, D), BF16),
                  jax.ShapeDtypeStruct((b, s, SSD_INNER), BF16),
                  jax.ShapeDtypeStruct((b, s, CONV_DIM), BF16),
                  jax.ShapeDtypeStruct((b, s, LANES), F32)]
    consts = [p[k] for k in ("mix_norm", "w_uv", "w_pin", "w_z", "w_xbc", "w_dt", "w_gate", "b_gate", "sgu_ln_g",
                             "sgu_ln_b", "sgu_ws", "sgu_bias", "pool_w", "pool_scale", "conv_w", "conv_b",
                             "dt_bias", "w_branch_a", "w_branch_b")]
    return pl.pallas_call(
        functools.partial(_mixin_body, seq_len=s),
        grid=(b, nt),
        in_specs=[tile(D),
                  pl.BlockSpec((1, HALO, D), lambda i, j: (i, jnp.maximum(j * hb - 1, 0), 0)),
                  pl.BlockSpec((1, HALO, D), lambda i, j: (i, jnp.minimum((j + 1) * hb, nt * hb - 1), 0))]
                 + [_layer_spec(c, l) for c in consts],
        out_specs=[tile(D), tile(D), tile(SSD_INNER), tile(CONV_DIM), tile(LANES)],
        out_shape=out_shapes,
        scratch_shapes=[pltpu.VMEM((POOL_W // LANES, tm + 2 * HALO, LANES), F32),
                        pltpu.VMEM((CONV_DIM // LANES, tm + 2 * HALO, LANES), F32),
                        pltpu.VMEM((tm, SGU_W), BF16),
                        pltpu.VMEM((tm, POOL_W), BF16),
                        pltpu.VMEM((tm, 2 * SGU_W), F32),
                        pltpu.VMEM((tm, 3 * D), F32),
                        pltpu.VMEM((CONV_DIM // LANES, tm, LANES), F32),
                        pltpu.VMEM((POOL_W // LANES, tm, LANES), F32)],
        compiler_params=_cparams(2),
        name="mixin",
    )(x, x, x, *consts)


def _pair_masks():
    lane = lax.broadcasted_iota(jnp.int32, (CHUNK // 2, LANES), 1)
    ones = jnp.uint32(0xFFFFFFFF)
    zero = jnp.uint32(0)
    return jnp.where(lane < HEADDIM, ones, zero), jnp.where(lane >= HEADDIM, ones, zero)


def _pair_blockdiag(v, mlo, mhi):
    bits = pltpu.bitcast(v, jnp.uint32)
    return jnp.concatenate([pltpu.bitcast(bits & mlo, BF16), pltpu.bitcast(bits & mhi, BF16)], axis=0)


def _state_body(xcf_ref, dtf_ref, xcb_ref, dtb_ref, alog_ref, trit_ref, of_ref, ob_ref, h_ref):
    t = pl.program_id(1)
    nc = xcf_ref.shape[1] // CHUNK

    @pl.when(t == 0)
    def _():
        h_ref[...] = jnp.zeros_like(h_ref)

    mlo, mhi = _pair_masks()
    low_half = lax.broadcasted_iota(jnp.int32, (1, LANES), 1) < HEADDIM
    sts, cds = [], []
    for d, (xc_ref, dt_ref) in enumerate(((xcf_ref, dtf_ref), (xcb_ref, dtb_ref))):
        a_col = -jnp.exp(alog_ref[d])
        trit = trit_ref[d]
        sts.append([])
        cds.append([])
        for ci in range(nc):
            rs = slice(ci * CHUNK, (ci + 1) * CHUNK)
            dtt = dt_ref[0, rs, :].T[d * HEADS:(d + 1) * HEADS, :]
            dat = dtt * a_col
            hi, mid, lo = _split3(dat)
            r = _dot(jnp.concatenate([hi, mid, lo], axis=0), trit)
            rest = r[0:HEADS] + r[HEADS:2 * HEADS] + r[2 * HEADS:]
            wt = dtt * jnp.exp(rest)
            cds[d].append(jnp.exp(jnp.sum(dat, axis=1, keepdims=True)))
            bmt = xc_ref[0, rs, SSD_INNER:SSD_INNER + GN].astype(F32).T
            per_pair = []
            for j in range(HEADS // 2):
                g = (2 * j) // (HEADS // GROUPS)
                bg = bmt[g * STATE:(g + 1) * STATE, :]
                lhs = jnp.concatenate([bg * wt[2 * j:2 * j + 1, :], bg * wt[2 * j + 1:2 * j + 2, :]],
                                      axis=1).astype(BF16)
                rhs = _pair_blockdiag(xc_ref[0, rs, j * LANES:(j + 1) * LANES], mlo, mhi)
                per_pair.append(_dot(lhs, rhs))
            sts[d].append(per_pair)

    for d, o_ref in enumerate((of_ref, ob_ref)):
        order = range(nc) if d == 0 else range(nc - 1, -1, -1)
        for j in range(HEADS // 2):
            h = h_ref[d, j]
            for ci in order:
                o_ref[0, ci, j] = h.astype(BF16)
                cd = cds[d][ci]
                h = h * jnp.where(low_half, cd[2 * j:2 * j + 1, :], cd[2 * j + 1:2 * j + 2, :]) + sts[d][ci][j]
            h_ref[d, j] = h


def _state(xc, dt, p, l):
    b, s, _ = xc.shape
    ts = TS_STATE
    nt = s // ts
    nc = ts // CHUNK
    npair = HEADS // 2
    fwd = lambda w: pl.BlockSpec((1, ts, w), lambda i, j: (i, j, 0))
    bwd = lambda w: pl.BlockSpec((1, ts, w), lambda i, j: (i, nt - 1 - j, 0))
    oshape = jax.ShapeDtypeStruct((b, s // CHUNK, npair, STATE, LANES), BF16)
    return pl.pallas_call(
        _state_body,
        grid=(b, nt),
        in_specs=[fwd(CONV_DIM), fwd(LANES), bwd(CONV_DIM), bwd(LANES),
                  _layer_spec(p["alog_heads"], l), _const_spec(p["tri_strict_t"].shape)],
        out_specs=[pl.BlockSpec((1, nc, npair, STATE, LANES), lambda i, j: (i, j, 0, 0, 0)),
                   pl.BlockSpec((1, nc, npair, STATE, LANES), lambda i, j: (i, nt - 1 - j, 0, 0, 0))],
        out_shape=[oshape, oshape],
        scratch_shapes=[pltpu.VMEM((2, npair, STATE, LANES), F32)],
        compiler_params=_cparams(2),
        name="ssd_state",
    )(xc, dt, xc, dt, p["alog_heads"], p["tri_strict_t"])


def _column(mat, h):
    return jnp.broadcast_to(mat[:, h:h + 1], mat.shape)


def _mix_body(x_ref, ab_ref, g2_ref, z_ref, xc_ref, dt_ref, pf_ref, pb_ref, alog_ref, tri_ref,
              dskip_ref, normg_ref, wc_ref, wout_ref, o_ref, ybuf_ref, cin_ref):
    ts = x_ref.shape[1]
    nc = ts // CHUNK
    a_row = -jnp.exp(alog_ref[0:1, :]) * alog_ref[1:2, :]
    tri = tri_ref[...]
    lane = lax.broadcasted_iota(jnp.int32, (CHUNK, LANES), 1)
    rowi = lax.broadcasted_iota(jnp.int32, (CHUNK, LANES), 0)
    below = lane < rowi
    above = lane > rowi
    low_half = lane < HEADDIM
    mlo, mhi = _pair_masks()
    half = SSD_INNER // GROUPS

    def chunk(c, carry):
        r0 = pl.multiple_of(c * CHUNK, CHUNK)
        dt = dt_ref[0, pl.ds(r0, CHUNK), :]
        hi, mid, lo = _split3(dt * a_row)
        r = _dot(tri, jnp.concatenate([hi, mid, lo], axis=1))
        cum = r[:, :LANES] + r[:, LANES:2 * LANES] + r[:, 2 * LANES:]
        pcol = jnp.where(lane < HEADS, cum[:CHUNK], cum[CHUNK:]) * LOG2E
        prow = pcol.T
        dtrow = dt.T
        rsub = prow[0:2 * HEADS] - jnp.log2(dtrow[0:2 * HEADS])
        rdiag = jnp.log2(dtrow[0:HEADS] + dtrow[HEADS:2 * HEADS])

        cmf = xc_ref[0, pl.ds(r0, CHUNK), SSD_INNER + GN:].astype(F32)
        bm = xc_ref[0, pl.ds(r0, CHUNK), SSD_INNER:SSD_INNER + GN]
        cmr = pltpu.roll(cmf, STATE, axis=1)
        cmg = jnp.concatenate([jnp.where(low_half, cmf, 0.0), jnp.where(low_half, 0.0, cmf)], axis=0).astype(BF16)
        cb2 = lax.dot_general(cmg, bm, (((1,), (1,)), ((), ())), preferred_element_type=F32)
        cbs = [cb2[:CHUNK], cb2[CHUNK:]]
        cdup = [jnp.where(low_half, cmf, cmr), jnp.where(low_half, cmr, cmf)]

        for j in range(HEADS // 2):
            g = (2 * j) // (HEADS // GROUPS)
            ms, ces = [], []
            for h in (2 * j, 2 * j + 1):
                colf = _column(pcol, h)
                colb = _column(pcol, HEADS + h)
                ex = jnp.where(below, colf - rsub[h:h + 1, :],
                               jnp.where(above, colb - rsub[HEADS + h:HEADS + h + 1, :], rdiag[h:h + 1, :]))
                ms.append((cbs[g] * jnp.exp2(ex)).astype(BF16))
                ces.append((cdup[g] * jnp.exp2(jnp.where(low_half, colf, colb))).astype(BF16))
            xp = xc_ref[0, pl.ds(r0, CHUNK), j * LANES:(j + 1) * LANES]
            st = jnp.concatenate([pf_ref[0, c, j], pb_ref[0, c, j]], axis=0)
            lhs = jnp.concatenate(ms + ces, axis=1)
            rhs = jnp.concatenate([_pair_blockdiag(xp, mlo, mhi), _pair_blockdiag(st, mlo, mhi)], axis=0)
            y = _dot(lhs, rhs) + xp.astype(F32) * dskip_ref[:, j * LANES:(j + 1) * LANES]
            ybuf_ref[pl.ds(r0, CHUNK), j * LANES:(j + 1) * LANES] = y
        return carry

    lax.fori_loop(0, nc, chunk, 0, unroll=2)

    for ci in range(nc):
        rs = slice(ci * CHUNK, (ci + 1) * CHUNK)
        z = z_ref[0, rs, :].astype(F32)
        y = ybuf_ref[rs, :] * (z * _sigmoid(z))
        for g in range(GROUPS):
            cs = slice(g * half, (g + 1) * half)
            cin_ref[rs, cs] = _rms(y[:, cs], normg_ref[:, cs]).astype(BF16)

    cproj = _dot(cin_ref[...], wc_ref[...])
    merged = ab_ref[0].astype(F32) + g2_ref[0].astype(F32) * cproj
    o_ref[0] = x_ref[0] + _dot(merged.astype(BF16), wout_ref[...])


def _mix(x, ab, g2, z, xc, dt, prev_f, prev_b, p, l):
    b, s, _ = x.shape
    ts = TS_MIX
    nc = ts // CHUNK
    npair = HEADS // 2
    tile = lambda w: pl.BlockSpec((1, ts, w), lambda i, j: (i, j, 0))
    prev = pl.BlockSpec((1, nc, npair, STATE, LANES), lambda i, j: (i, j, 0, 0, 0))
    consts = [p["alog_all"], p["tri_incl"], p["d_skip"], p["ssd_norm"], p["w_branch_c"], p["w_out"]]
    return pl.pallas_call(
        _mix_body,
        grid=(b, s // ts),
        in_specs=[tile(D), tile(D), tile(D), tile(SSD_INNER), tile(CONV_DIM), tile(LANES), prev, prev]
                 + [_const_spec(c.shape) if c is p["tri_incl"] else _layer_spec(c, l) for c in consts],
        out_specs=tile(D),
        out_shape=jax.ShapeDtypeStruct((b, s, D), F32),
        scratch_shapes=[pltpu.VMEM((ts, SSD_INNER), F32), pltpu.VMEM((ts, SSD_INNER), BF16)],
        compiler_params=_cparams(2),
        name="ssd_mix",
    )(x, ab, g2, z, xc, dt, prev_f, prev_b, *consts)


def _kv_body(m_ref, g_ref, wkv_ref, k_ref, v_ref):
    mn = _rms(m_ref[0], g_ref[...]).astype(BF16)
    kv = _dot(mn, wkv_ref[...])
    k_ref[0] = kv[:, :D].astype(BF16)
    v_ref[0] = kv[:, D:].astype(BF16)


def _kv(mem, p, l):
    b, m, _ = mem.shape
    blk = pl.BlockSpec((1, m, D), lambda i: (i, 0, 0))
    return pl.pallas_call(
        _kv_body,
        grid=(b,),
        in_specs=[blk, _layer_spec(p["mem_norm"], l), _layer_spec(p["xattn_wkv"], l)],
        out_specs=[blk, blk],
        out_shape=[jax.ShapeDtypeStruct((b, m, D), BF16)] * 2,
        compiler_params=_cparams(1),
        name="xattn_kv",
    )(mem, p["mem_norm"], p["xattn_wkv"])


def _attn_body(x_ref, k_ref, v_ref, g_ref, wq_ref, wo_ref, o_ref, obuf_ref):
    x = x_ref[0]
    xn = _rms(x, g_ref[...]).astype(BF16)
    q = (_dot(xn, wq_ref[...]) * (XA_HD ** -0.5)).astype(BF16)
    for h in range(XA_HEADS):
        cs = slice(h * XA_HD, (h + 1) * XA_HD)
        s = lax.dot_general(q[:, cs], k_ref[0, :, cs], (((1,), (1,)), ((), ())),
                            preferred_element_type=F32)
        e = jnp.exp(s - jnp.max(s, axis=-1, keepdims=True))
        o = _dot(e.astype(BF16), v_ref[0, :, cs]) * (1.0 / jnp.sum(e, axis=-1, keepdims=True))
        obuf_ref[:, cs] = o.astype(BF16)
    o_ref[0] = x + _dot(obuf_ref[...], wo_ref[...])


def _attn(x, k, v, p, l):
    b, s, _ = x.shape
    m = k.shape[1]
    tm = TM_ATTN
    tile = pl.BlockSpec((1, tm, D), lambda i, j: (i, j, 0))
    kvb = pl.BlockSpec((1, m, D), lambda i, j: (i, 0, 0))
    return pl.pallas_call(
        _attn_body,
        grid=(b, s // tm),
        in_specs=[tile, kvb, kvb, _layer_spec(p["xattn_norm"], l), _layer_spec(p["xattn_wq"], l),
                  _layer_spec(p["xattn_wo"], l)],
        out_specs=tile,
        out_shape=jax.ShapeDtypeStruct((b, s, D), F32),
        scratch_shapes=[pltpu.VMEM((tm, D), BF16)],
        compiler_params=_cparams(2),
        name="xattn",
    )(x, k, v, p["xattn_norm"], p["xattn_wq"], p["xattn_wo"])


def _constants():
    q = np.arange(CHUNK)
    lower_incl = (q[None, :] <= q[:, None]).astype(np.float32)
    upper_incl = (q[None, :] >= q[:, None]).astype(np.float32)
    strict_up = (q[None, :] > q[:, None]).astype(np.float32)
    strict_lo = (q[None, :] < q[:, None]).astype(np.float32)
    return dict(
        tri_incl=jnp.asarray(np.concatenate([lower_incl, upper_incl], axis=0), BF16),
        tri_strict_t=jnp.asarray(np.stack([strict_up.T, strict_lo.T]), BF16),
    )


def _prepare(raw):
    L = raw["w_in"].shape[0]
    w_in = raw["w_in"]
    o_pin = 2 * SGU_W
    o_z = o_pin + POOL_W
    o_xbc = o_z + SSD_INNER
    o_dt = o_xbc + CONV_DIM
    row = lambda a: a.reshape(L, 1, -1).astype(F32)
    p = dict(_constants())
    for k in ("ffn1_norm", "ffn2_norm", "mix_norm", "xattn_norm", "mem_norm", "b_gate", "sgu_ln_g", "sgu_ln_b",
              "pool_scale", "conv_b", "ssd_norm"):
        p[k] = row(raw[k])
    p["final_norm"] = raw["final_norm"].reshape(1, D).astype(F32)
    for k in ("ffn1_w13", "ffn1_w2", "ffn2_w13", "ffn2_w2", "w_gate", "sgu_ws", "pool_w", "w_branch_a",
              "w_branch_b", "w_branch_c", "w_out", "xattn_wq", "xattn_wkv", "xattn_wo"):
        p[k] = raw[k].astype(BF16)
    p["w_uv"] = w_in[:, :, :o_pin].astype(BF16)
    p["w_pin"] = w_in[:, :, o_pin:o_z].astype(BF16)
    p["w_z"] = w_in[:, :, o_z:o_xbc].astype(BF16)
    p["w_xbc"] = w_in[:, :, o_xbc:o_dt].astype(BF16)
    p["w_dt"] = jnp.pad(w_in[:, :, o_dt:], ((0, 0), (0, 0), (0, LANES - 2 * HEADS))).astype(BF16)
    p["dt_bias"] = jnp.pad(raw["dt_bias"].reshape(L, 1, 2 * HEADS), ((0, 0), (0, 0), (0, LANES - 2 * HEADS))).astype(F32)
    p["conv_w"] = raw["conv_w"].astype(F32)
    p["sgu_bias"] = jnp.broadcast_to(raw["sgu_bias"][:, :, :, None], (L, SGU_G, CHUNK, LANES)).astype(F32)
    alog = raw["a_log"].astype(F32)
    lanes_used = jnp.pad(alog.reshape(L, 1, 2 * HEADS), ((0, 0), (0, 0), (0, LANES - 2 * HEADS)))
    lanes_mask = jnp.pad(jnp.ones((L, 1, 2 * HEADS), F32), ((0, 0), (0, 0), (0, LANES - 2 * HEADS)))
    p["alog_all"] = jnp.concatenate([lanes_used, lanes_mask], axis=1)
    p["alog_heads"] = jnp.broadcast_to(alog[:, :, :, None], (L, 2, HEADS, LANES))
    p["d_skip"] = jnp.repeat(raw["d_skip"].astype(F32), HEADDIM, axis=-1).reshape(L, 1, SSD_INNER)
    return p, L


def _trunk(x, mem, p, depth):
    b, s, _ = x.shape
    for l in range(depth):
        xf = _ffn(x.reshape(b * s, D), p["ffn1_norm"], p["ffn1_w13"], p["ffn1_w2"],
                  p["final_norm"], False, l)
        x = xf.reshape(b, s, D)
        ab, g2, z, xc, dt = _mixin(x, p, l)
        prev_f, prev_b = _state(xc, dt, p, l)
        x = _mix(x, ab, g2, z, xc, dt, prev_f, prev_b, p, l)
        k, v = _kv(mem, p, l)
        x = _attn(x, k, v, p, l)
        xf = _ffn(x.reshape(b * s, D), p["ffn2_norm"], p["ffn2_w13"], p["ffn2_w2"],
                  p["final_norm"], l == depth - 1, l)
        x = xf.reshape(b, s, D)
    return x


def kernel(x_prompt, x_sample, mem_prompt, mem_sample, ffn1_norm, ffn1_w13, ffn1_w2, mix_norm, w_in, w_gate, b_gate, sgu_ln_g, sgu_ln_b, sgu_ws, sgu_bias, pool_w, pool_scale, conv_w, conv_b, dt_bias, a_log, d_skip, ssd_norm, w_branch_a, w_branch_b, w_branch_c, w_out, xattn_norm, mem_norm, xattn_wq, xattn_wkv, xattn_wo, ffn2_norm, ffn2_w13, ffn2_w2, final_norm):
    raw = dict(ffn1_norm=ffn1_norm, ffn1_w13=ffn1_w13, ffn1_w2=ffn1_w2, mix_norm=mix_norm, w_in=w_in,
               w_gate=w_gate, b_gate=b_gate, sgu_ln_g=sgu_ln_g, sgu_ln_b=sgu_ln_b, sgu_ws=sgu_ws,
               sgu_bias=sgu_bias, pool_w=pool_w, pool_scale=pool_scale, conv_w=conv_w, conv_b=conv_b,
               dt_bias=dt_bias, a_log=a_log, d_skip=d_skip, ssd_norm=ssd_norm, w_branch_a=w_branch_a,
               w_branch_b=w_branch_b, w_branch_c=w_branch_c, w_out=w_out, xattn_norm=xattn_norm,
               mem_norm=mem_norm, xattn_wq=xattn_wq, xattn_wkv=xattn_wkv, xattn_wo=xattn_wo,
               ffn2_norm=ffn2_norm, ffn2_w13=ffn2_w13, ffn2_w2=ffn2_w2, final_norm=final_norm)
    p, depth = _prepare(raw)
    return (_trunk(x_prompt, mem_prompt, p, depth), _trunk(x_sample, mem_sample, p, depth))
```

```python
import functools

import numpy as np
import jax
import jax.numpy as jnp
from jax import lax
from jax.experimental import pallas as pl
from jax.experimental.pallas import tpu as pltpu

F32 = jnp.float32
BF16 = jnp.bfloat16

D = 1024
EPS = 1e-6
D_FF = 2816
SGU_W = 512
SGU_G = 4
CHUNK = 128
POOL_W = 512
POOL_WINDOWS = (2, 4, 8, 16)
SSD_INNER = 1024
HEADDIM = 64
HEADS = 16
GROUPS = 2
STATE = 64
GN = GROUPS * STATE
CONV_DIM = SSD_INNER + 2 * GN
CONV_K = 4
CONV_LEFT = 2
XA_HEADS = 4
XA_HD = D // XA_HEADS
LANES = 128
LOG2E = 1.4426950408889634
HALO = 16

TM_FFN = 1024
TM_MIXIN = 512
TS_STATE = 1024
TS_MIX = 1024
TM_ATTN = 1024
FFN_FC = 256

V7X_VMEM_BYTES = 64 * 1024 * 1024
VMEM_LIMIT = V7X_VMEM_BYTES - 8 * 1024 * 1024


def _cparams(n_axes):
    return pltpu.CompilerParams(dimension_semantics=("arbitrary",) * n_axes,
                                vmem_limit_bytes=VMEM_LIMIT)


def _const_spec(shape):
    nd = len(shape)
    return pl.BlockSpec(shape, lambda *_: (0,) * nd, pipeline_mode=pl.Buffered(1))


def _layer_spec(stacked, l):
    nd = stacked.ndim - 1
    return pl.BlockSpec((None,) + stacked.shape[1:], lambda *_: (l,) + (0,) * nd,
                        pipeline_mode=pl.Buffered(1))


def _rms(x, g):
    return x * lax.rsqrt(jnp.mean(x * x, axis=-1, keepdims=True) + EPS) * g


def _sigmoid(x):
    return 1.0 / (1.0 + jnp.exp2(x * (-LOG2E)))


def _split3(x):
    hi = x.astype(BF16)
    r1 = x - hi.astype(F32)
    mid = r1.astype(BF16)
    lo = (r1 - mid.astype(F32)).astype(BF16)
    return hi, mid, lo


def _dot(a, b):
    return jnp.dot(a, b, preferred_element_type=F32)


def _ffn_body(x_ref, g_ref, w13_ref, w2_ref, fin_ref, o_ref, gbuf_ref, *, final):
    x = x_ref[...]
    xn = _rms(x, g_ref[...]).astype(BF16)
    for c in range(D_FF // FFN_FC):
        a = _dot(xn, w13_ref[:, c * FFN_FC:(c + 1) * FFN_FC])
        b = _dot(xn, w13_ref[:, D_FF + c * FFN_FC:D_FF + (c + 1) * FFN_FC])
        gbuf_ref[:, c * FFN_FC:(c + 1) * FFN_FC] = (a * _sigmoid(a) * b).astype(BF16)
    y = x + 0.5 * _dot(gbuf_ref[...], w2_ref[...])
    if final:
        y = _rms(y, fin_ref[...])
    o_ref[...] = y


def _ffn(x, g, w13, w2, fin, final, l):
    t = x.shape[0]
    tm = TM_FFN
    return pl.pallas_call(
        functools.partial(_ffn_body, final=final),
        grid=(t // tm,),
        in_specs=[pl.BlockSpec((tm, D), lambda i: (i, 0)),
                  _layer_spec(g, l), _layer_spec(w13, l), _layer_spec(w2, l), _const_spec((1, D))],
        out_specs=pl.BlockSpec((tm, D), lambda i: (i, 0)),
        out_shape=jax.ShapeDtypeStruct((t, D), F32),
        scratch_shapes=[pltpu.VMEM((tm, D_FF), BF16)],
        compiler_params=_cparams(1),
        name="ffn_final" if final else "ffn",
    )(x, g, w13, w2, fin)


def _mixin_body(x_ref, xp_ref, xn_ref, g_ref, wuv_ref, wpin_ref, wz_ref, wxbc_ref, wdt_ref,
                wgate_ref, bgate_ref, lng_ref, lnb_ref, ws_ref, sbias_ref, poolw_ref, pscale_ref,
                convw_ref, convb_ref, dtb_ref, wa_ref, wb_ref,
                ab_ref, g2_ref, z_ref, xc_ref, dt_ref,
                pin_ext, xbc_ext, a_buf, b_buf, uv_buf, graw_buf, conv_out, pool_out, *, seq_len):
    tm = x_ref.shape[1]
    t = pl.program_id(1)
    nt = pl.num_programs(1)
    g = g_ref[...]
    xp = jnp.where(t > 0, xp_ref[0], 0.0)
    xn = jnp.where(t < nt - 1, xn_ref[0], 0.0)
    n_main = _rms(x_ref[0], g).astype(BF16)
    n_prev = _rms(xp, g).astype(BF16)
    n_next = _rms(xn, g).astype(BF16)

    n_ext = jnp.concatenate([n_prev, n_main, n_next], axis=0)
    pin = _dot(n_ext, wpin_ref[...])
    for c in range(POOL_W // LANES):
        pin_ext[c] = pin[:, c * LANES:(c + 1) * LANES]
    xbc = _dot(n_ext, wxbc_ref[...])
    for c in range(CONV_DIM // LANES):
        xbc_ext[c] = xbc[:, c * LANES:(c + 1) * LANES]
    uv_buf[...] = _dot(n_main, wuv_ref[...])
    graw_buf[...] = _dot(n_main, wgate_ref[...])
    z_ref[0] = _dot(n_main, wz_ref[...]).astype(BF16)
    raw = _dot(n_main, wdt_ref[...]) + dtb_ref[...]

    hm = tm // 2

    sp = jnp.maximum(raw, 0.0) + jnp.log(1.0 + jnp.exp(-jnp.abs(raw)))
    lane = lax.broadcasted_iota(jnp.int32, (tm, LANES), 1)
    dt_ref[0] = jnp.where(lane < 2 * HEADS, sp, 0.0)

    pos0 = t * tm + 2 * lax.broadcasted_iota(jnp.int32, (hm, LANES), 0)
    for gi, w in enumerate(POOL_WINDOWS):
        cs = slice(gi * LANES, (gi + 1) * LANES)
        for e in range(2):
            s = pin_ext[gi, pl.ds(HALO - w // 2 + e, hm, stride=2), :]
            for k in range(1, w):
                s = s + pin_ext[gi, pl.ds(HALO - w // 2 + k + e, hm, stride=2), :]
            pos = pos0 + e
            lo = jnp.clip(pos - w // 2, 0, seq_len)
            hi = jnp.clip(pos + w - w // 2, 0, seq_len)
            cnt = (hi - lo).astype(F32)
            pool_out[gi, pl.ds(e, hm, stride=2), :] = s / cnt - pin_ext[gi, pl.ds(HALO + e, hm, stride=2), :]
        b_buf[:, cs] = (_dot(pool_out[gi].astype(BF16), poolw_ref[gi]) * pscale_ref[:, cs]).astype(BF16)

    def gelu(v):
        return v * _sigmoid(v * (1.5957691216057308 + 0.07135481627260025 * (v * v)))

    vns = []
    for c in range(tm // CHUNK):
        v = gelu(uv_buf[c * CHUNK:(c + 1) * CHUNK, SGU_W:])
        vc = v - jnp.mean(v, axis=-1, keepdims=True)
        var = jnp.mean(vc * vc, axis=-1, keepdims=True)
        vns.append((vc * lax.rsqrt(var + EPS) * lng_ref[...] + lnb_ref[...]).astype(BF16))
    for gi in range(SGU_G):
        cs = slice(gi * LANES, (gi + 1) * LANES)
        mixed = _dot(ws_ref[gi], jnp.concatenate([vn[:, cs] for vn in vns], axis=1))
        for c in range(tm // CHUNK):
            rs = slice(c * CHUNK, (c + 1) * CHUNK)
            a_buf[rs, cs] = (gelu(uv_buf[rs, cs]) * (mixed[:, c * LANES:(c + 1) * LANES] + sbias_ref[gi])).astype(BF16)

    for c in range(CONV_DIM // LANES):
        cs = slice(c * LANES, (c + 1) * LANES)
        for e in range(2):
            acc = convb_ref[:, cs] + xbc_ext[c, pl.ds(HALO - CONV_LEFT + e, hm, stride=2), :] * convw_ref[0:1, cs]
            for k in range(1, CONV_K):
                acc = acc + xbc_ext[c, pl.ds(HALO - CONV_LEFT + k + e, hm, stride=2), :] * convw_ref[k:k + 1, cs]
            conv_out[c, pl.ds(e, hm, stride=2), :] = acc * _sigmoid(acc)
        xc_ref[0, :, cs] = conv_out[c].astype(BF16)

    gates = _sigmoid(graw_buf[...] + bgate_ref[...])
    ab = gates[:, :D] * _dot(a_buf[...], wa_ref[...]) + gates[:, D:2 * D] * _dot(b_buf[...], wb_ref[...])
    ab_ref[0] = ab.astype(BF16)
    g2_ref[0] = gates[:, 2 * D:].astype(BF16)


def _mixin(x, p, l):
    b, s, _ = x.shape
    tm = TM_MIXIN
    nt = s // tm
    hb = tm // HALO
    tile = lambda w: pl.BlockSpec((1, tm, w), lambda i, j: (i, j, 0))
    out_shapes = [jax.ShapeDtypeStruct((b, s, D), BF16),
                  jax.ShapeDtypeStruct((b, s, D), BF16),
                  jax.ShapeDtypeStruct((b, s, SSD_INNER), BF16),
                  jax.ShapeDtypeStruct((b, s, CONV_DIM), BF16),
                  jax.ShapeDtypeStruct((b, s, LANES), F32)]
    consts = [p[k] for k in ("mix_norm", "w_uv", "w_pin", "w_z", "w_xbc", "w_dt", "w_gate", "b_gate", "sgu_ln_g",
                             "sgu_ln_b", "sgu_ws", "sgu_bias", "pool_w", "pool_scale", "conv_w", "conv_b",
                             "dt_bias", "w_branch_a", "w_branch_b")]
    return pl.pallas_call(
        functools.partial(_mixin_body, seq_len=s),
        grid=(b, nt),
        in_specs=[tile(D),
                  pl.BlockSpec((1, HALO, D), lambda i, j: (i, jnp.maximum(j * hb - 1, 0), 0)),
                  pl.BlockSpec((1, HALO, D), lambda i, j: (i, jnp.minimum((j + 1) * hb, nt * hb - 1), 0))]
                 + [_layer_spec(c, l) for c in consts],
        out_specs=[tile(D), tile(D), tile(SSD_INNER), tile(CONV_DIM), tile(LANES)],
        out_shape=out_shapes,
        scratch_shapes=[pltpu.VMEM((POOL_W // LANES, tm + 2 * HALO, LANES), F32),
                        pltpu.VMEM((CONV_DIM // LANES, tm + 2 * HALO, LANES), F32),
                        pltpu.VMEM((tm, SGU_W), BF16),
                        pltpu.VMEM((tm, POOL_W), BF16),
                        pltpu.VMEM((tm, 2 * SGU_W), F32),
                        pltpu.VMEM((tm, 3 * D), F32),
                        pltpu.VMEM((CONV_DIM // LANES, tm, LANES), F32),
                        pltpu.VMEM((POOL_W // LANES, tm, LANES), F32)],
        compiler_params=_cparams(2),
        name="mixin",
    )(x, x, x, *consts)


def _pair_masks():
    lane = lax.broadcasted_iota(jnp.int32, (CHUNK // 2, LANES), 1)
    ones = jnp.uint32(0xFFFFFFFF)
    zero = jnp.uint32(0)
    return jnp.where(lane < HEADDIM, ones, zero), jnp.where(lane >= HEADDIM, ones, zero)


def _pair_blockdiag(v, mlo, mhi):
    bits = pltpu.bitcast(v, jnp.uint32)
    return jnp.concatenate([pltpu.bitcast(bits & mlo, BF16), pltpu.bitcast(bits & mhi, BF16)], axis=0)


def _state_body(xcf_ref, dtf_ref, xcb_ref, dtb_ref, alog_ref, trit_ref, of_ref, ob_ref, h_ref):
    t = pl.program_id(1)
    nc = xcf_ref.shape[1] // CHUNK

    @pl.when(t == 0)
    def _():
        h_ref[...] = jnp.zeros_like(h_ref)

    mlo, mhi = _pair_masks()
    low_half = lax.broadcasted_iota(jnp.int32, (1, LANES), 1) < HEADDIM
    tiles = ((xcf_ref, dtf_ref), (xcb_ref, dtb_ref))
    wts, cds = [], []
    for d, (xc_ref, dt_ref) in enumerate(tiles):
        a_col = -jnp.exp(alog_ref[d])
        trit = trit_ref[d]
        wts.append([])
        cds.append([])
        for ci in range(nc):
            rs = slice(ci * CHUNK, (ci + 1) * CHUNK)
            dtt = dt_ref[0, rs, :].T[d * HEADS:(d + 1) * HEADS, :]
            dat = dtt * a_col
            hi, mid, lo = _split3(dat)
            r = _dot(jnp.concatenate([hi, mid, lo], axis=0), trit)
            rest = r[0:HEADS] + r[HEADS:2 * HEADS] + r[2 * HEADS:]
            wts[d].append(dtt * jnp.exp(rest))
            cds[d].append(jnp.exp(jnp.sum(dat, axis=1, keepdims=True)))

    sts = []
    for d, (xc_ref, dt_ref) in enumerate(tiles):
        sts.append([])
        for ci in range(nc):
            rs = slice(ci * CHUNK, (ci + 1) * CHUNK)
            wt = wts[d][ci]
            bmt = xc_ref[0, rs, SSD_INNER:SSD_INNER + GN].astype(F32).T
            per_pair = []
            for j in range(HEADS // 2):
                g = (2 * j) // (HEADS // GROUPS)
                bg = bmt[g * STATE:(g + 1) * STATE, :]
                lhs = jnp.concatenate([bg * wt[2 * j:2 * j + 1, :], bg * wt[2 * j + 1:2 * j + 2, :]],
                                      axis=1).astype(BF16)
                rhs = _pair_blockdiag(xc_ref[0, rs, j * LANES:(j + 1) * LANES], mlo, mhi)
                per_pair.append(_dot(lhs, rhs))
            sts[d].append(per_pair)

    for d, o_ref in enumerate((of_ref, ob_ref)):
        order = range(nc) if d == 0 else range(nc - 1, -1, -1)
        for j in range(HEADS // 2):
            h = h_ref[d, j]
            for ci in order:
                o_ref[0, ci, j] = h.astype(BF16)
                cd = cds[d][ci]
                h = h * jnp.where(low_half, cd[2 * j:2 * j + 1, :], cd[2 * j + 1:2 * j + 2, :]) + sts[d][ci][j]
            h_ref[d, j] = h


def _state(xc, dt, p, l):
    b, s, _ = xc.shape
    ts = TS_STATE
    nt = s // ts
    nc = ts // CHUNK
    npair = HEADS // 2
    fwd = lambda w: pl.BlockSpec((1, ts, w), lambda i, j: (i, j, 0))
    bwd = lambda w: pl.BlockSpec((1, ts, w), lambda i, j: (i, nt - 1 - j, 0))
    oshape = jax.ShapeDtypeStruct((b, s // CHUNK, npair, STATE, LANES), BF16)
    return pl.pallas_call(
        _state_body,
        grid=(b, nt),
        in_specs=[fwd(CONV_DIM), fwd(LANES), bwd(CONV_DIM), bwd(LANES),
                  _layer_spec(p["alog_heads"], l), _const_spec(p["tri_strict_t"].shape)],
        out_specs=[pl.BlockSpec((1, nc, npair, STATE, LANES), lambda i, j: (i, j, 0, 0, 0)),
                   pl.BlockSpec((1, nc, npair, STATE, LANES), lambda i, j: (i, nt - 1 - j, 0, 0, 0))],
        out_shape=[oshape, oshape],
        scratch_shapes=[pltpu.VMEM((2, npair, STATE, LANES), F32)],
        compiler_params=_cparams(2),
        name="ssd_state",
    )(xc, dt, xc, dt, p["alog_heads"], p["tri_strict_t"])


def _column(mat, h):
    return jnp.broadcast_to(mat[:, h:h + 1], mat.shape)


def _mix_body(x_ref, ab_ref, g2_ref, z_ref, xc_ref, dt_ref, pf_ref, pb_ref, alog_ref, tri_ref,
              dskip_ref, normg_ref, wc_ref, wout_ref, o_ref, ybuf_ref, cin_ref):
    ts = x_ref.shape[1]
    nc = ts // CHUNK
    a_row = -jnp.exp(alog_ref[0:1, :]) * alog_ref[1:2, :]
    tri = tri_ref[...]
    lane = lax.broadcasted_iota(jnp.int32, (CHUNK, LANES), 1)
    rowi = lax.broadcasted_iota(jnp.int32, (CHUNK, LANES), 0)
    below = lane < rowi
    above = lane > rowi
    low_half = lane < HEADDIM
    mlo, mhi = _pair_masks()
    half = SSD_INNER // GROUPS

    def decays(c):
        r0 = pl.multiple_of(c * CHUNK, CHUNK)
        dt = dt_ref[0, pl.ds(r0, CHUNK), :]
        hi, mid, lo = _split3(dt * a_row)
        r = _dot(tri, jnp.concatenate([hi, mid, lo], axis=1))
        cum = r[:, :LANES] + r[:, LANES:2 * LANES] + r[:, 2 * LANES:]
        pcol = jnp.where(lane < HEADS, cum[:CHUNK], cum[CHUNK:]) * LOG2E
        prow = pcol.T
        dtrow = dt.T
        rsub = prow[0:2 * HEADS] - jnp.log2(dtrow[0:2 * HEADS])
        rdiag = jnp.log2(dtrow[0:HEADS] + dtrow[HEADS:2 * HEADS])

        cmf = xc_ref[0, pl.ds(r0, CHUNK), SSD_INNER + GN:].astype(F32)
        bm = xc_ref[0, pl.ds(r0, CHUNK), SSD_INNER:SSD_INNER + GN]
        cmr = pltpu.roll(cmf, STATE, axis=1)
        cmg = jnp.concatenate([jnp.where(low_half, cmf, 0.0), jnp.where(low_half, 0.0, cmf)], axis=0).astype(BF16)
        cb2 = lax.dot_general(cmg, bm, (((1,), (1,)), ((), ())), preferred_element_type=F32)
        cbs = [cb2[:CHUNK], cb2[CHUNK:]]
        cdup = [jnp.where(low_half, cmf, cmr), jnp.where(low_half, cmr, cmf)]
        return pcol, rsub, rdiag, cbs, cdup

    def outputs(c, pcol, rsub, rdiag, cbs, cdup):
        r0 = pl.multiple_of(c * CHUNK, CHUNK)
        for j in range(HEADS // 2):
            g = (2 * j) // (HEADS // GROUPS)
            ms, ces = [], []
            for h in (2 * j, 2 * j + 1):
                colf = _column(pcol, h)
                colb = _column(pcol, HEADS + h)
                ex = jnp.where(below, colf - rsub[h:h + 1, :],
                               jnp.where(above, colb - rsub[HEADS + h:HEADS + h + 1, :], rdiag[h:h + 1, :]))
                ms.append((cbs[g] * jnp.exp2(ex)).astype(BF16))
                ces.append((cdup[g] * jnp.exp2(jnp.where(low_half, colf, colb))).astype(BF16))
            xp = xc_ref[0, pl.ds(r0, CHUNK), j * LANES:(j + 1) * LANES]
            st = jnp.concatenate([pf_ref[0, c, j], pb_ref[0, c, j]], axis=0)
            lhs = jnp.concatenate(ms + ces, axis=1)
            rhs = jnp.concatenate([_pair_blockdiag(xp, mlo, mhi), _pair_blockdiag(st, mlo, mhi)], axis=0)
            y = _dot(lhs, rhs) + xp.astype(F32) * dskip_ref[:, j * LANES:(j + 1) * LANES]
            ybuf_ref[pl.ds(r0, CHUNK), j * LANES:(j + 1) * LANES] = y

    group = 8

    def chunk_group(i, carry):
        staged = [decays(group * i + k) for k in range(group)]
        for k in range(group):
            outputs(group * i + k, *staged[k])
        return carry

    lax.fori_loop(0, nc // group, chunk_group, 0)

    for ci in range(nc):
        rs = slice(ci * CHUNK, (ci + 1) * CHUNK)
        z = z_ref[0, rs, :].astype(F32)
        y = ybuf_ref[rs, :] * (z * _sigmoid(z))
        for g in range(GROUPS):
            cs = slice(g * half, (g + 1) * half)
            cin_ref[rs, cs] = _rms(y[:, cs], normg_ref[:, cs]).astype(BF16)

    cproj = _dot(cin_ref[...], wc_ref[...])
    merged = ab_ref[0].astype(F32) + g2_ref[0].astype(F32) * cproj
    o_ref[0] = x_ref[0] + _dot(merged.astype(BF16), wout_ref[...])


def _mix(x, ab, g2, z, xc, dt, prev_f, prev_b, p, l):
    b, s, _ = x.shape
    ts = TS_MIX
    nc = ts // CHUNK
    npair = HEADS // 2
    tile = lambda w: pl.BlockSpec((1, ts, w), lambda i, j: (i, j, 0))
    prev = pl.BlockSpec((1, nc, npair, STATE, LANES), lambda i, j: (i, j, 0, 0, 0))
    consts = [p["alog_all"], p["tri_incl"], p["d_skip"], p["ssd_norm"], p["w_branch_c"], p["w_out"]]
    return pl.pallas_call(
        _mix_body,
        grid=(b, s // ts),
        in_specs=[tile(D), tile(D), tile(D), tile(SSD_INNER), tile(CONV_DIM), tile(LANES), prev, prev]
                 + [_const_spec(c.shape) if c is p["tri_incl"] else _layer_spec(c, l) for c in consts],
        out_specs=tile(D),
        out_shape=jax.ShapeDtypeStruct((b, s, D), F32),
        scratch_shapes=[pltpu.VMEM((ts, SSD_INNER), F32), pltpu.VMEM((ts, SSD_INNER), BF16)],
        compiler_params=_cparams(2),
        name="ssd_mix",
    )(x, ab, g2, z, xc, dt, prev_f, prev_b, *consts)


def _kv_body(m_ref, g_ref, wkv_ref, k_ref, v_ref):
    mn = _rms(m_ref[0], g_ref[...]).astype(BF16)
    kv = _dot(mn, wkv_ref[...])
    k_ref[0] = kv[:, :D].astype(BF16)
    v_ref[0] = kv[:, D:].astype(BF16)


def _kv(mem, p, l):
    b, m, _ = mem.shape
    blk = pl.BlockSpec((1, m, D), lambda i: (i, 0, 0))
    return pl.pallas_call(
        _kv_body,
        grid=(b,),
        in_specs=[blk, _layer_spec(p["mem_norm"], l), _layer_spec(p["xattn_wkv"], l)],
        out_specs=[blk, blk],
        out_shape=[jax.ShapeDtypeStruct((b, m, D), BF16)] * 2,
        compiler_params=_cparams(1),
        name="xattn_kv",
    )(mem, p["mem_norm"], p["xattn_wkv"])


def _attn_body(x_ref, k_ref, v_ref, g_ref, wq_ref, wo_ref, o_ref, obuf_ref):
    x = x_ref[0]
    xn = _rms(x, g_ref[...]).astype(BF16)
    q = (_dot(xn, wq_ref[...]) * (XA_HD ** -0.5)).astype(BF16)
    for h in range(XA_HEADS):
        cs = slice(h * XA_HD, (h + 1) * XA_HD)
        s = lax.dot_general(q[:, cs], k_ref[0, :, cs], (((1,), (1,)), ((), ())),
                            preferred_element_type=F32)
        e = jnp.exp(s - jnp.max(s, axis=-1, keepdims=True))
        o = _dot(e.astype(BF16), v_ref[0, :, cs]) * (1.0 / jnp.sum(e, axis=-1, keepdims=True))
        obuf_ref[:, cs] = o.astype(BF16)
    o_ref[0] = x + _dot(obuf_ref[...], wo_ref[...])


def _attn(x, k, v, p, l):
    b, s, _ = x.shape
    m = k.shape[1]
    tm = TM_ATTN
    tile = pl.BlockSpec((1, tm, D), lambda i, j: (i, j, 0))
    kvb = pl.BlockSpec((1, m, D), lambda i, j: (i, 0, 0))
    return pl.pallas_call(
        _attn_body,
        grid=(b, s // tm),
        in_specs=[tile, kvb, kvb, _layer_spec(p["xattn_norm"], l), _layer_spec(p["xattn_wq"], l),
                  _layer_spec(p["xattn_wo"], l)],
        out_specs=tile,
        out_shape=jax.ShapeDtypeStruct((b, s, D), F32),
        scratch_shapes=[pltpu.VMEM((tm, D), BF16)],
        compiler_params=_cparams(2),
        name="xattn",
    )(x, k, v, p["xattn_norm"], p["xattn_wq"], p["xattn_wo"])


def _constants():
    q = np.arange(CHUNK)
    lower_incl = (q[None, :] <= q[:, None]).astype(np.float32)
    upper_incl = (q[None, :] >= q[:, None]).astype(np.float32)
    strict_up = (q[None, :] > q[:, None]).astype(np.float32)
    strict_lo = (q[None, :] < q[:, None]).astype(np.float32)
    return dict(
        tri_incl=jnp.asarray(np.concatenate([lower_incl, upper_incl], axis=0), BF16),
        tri_strict_t=jnp.asarray(np.stack([strict_up.T, strict_lo.T]), BF16),
    )


def _prepare(raw):
    L = raw["w_in"].shape[0]
    w_in = raw["w_in"]
    o_pin = 2 * SGU_W
    o_z = o_pin + POOL_W
    o_xbc = o_z + SSD_INNER
    o_dt = o_xbc + CONV_DIM
    row = lambda a: a.reshape(L, 1, -1).astype(F32)
    p = dict(_constants())
    for k in ("ffn1_norm", "ffn2_norm", "mix_norm", "xattn_norm", "mem_norm", "b_gate", "sgu_ln_g", "sgu_ln_b",
              "pool_scale", "conv_b", "ssd_norm"):
        p[k] = row(raw[k])
    p["final_norm"] = raw["final_norm"].reshape(1, D).astype(F32)
    for k in ("ffn1_w13", "ffn1_w2", "ffn2_w13", "ffn2_w2", "w_gate", "sgu_ws", "pool_w", "w_branch_a",
              "w_branch_b", "w_branch_c", "w_out", "xattn_wq", "xattn_wkv", "xattn_wo"):
        p[k] = raw[k].astype(BF16)
    p["w_uv"] = w_in[:, :, :o_pin].astype(BF16)
    p["w_pin"] = w_in[:, :, o_pin:o_z].astype(BF16)
    p["w_z"] = w_in[:, :, o_z:o_xbc].astype(BF16)
    p["w_xbc"] = w_in[:, :, o_xbc:o_dt].astype(BF16)
    p["w_dt"] = jnp.pad(w_in[:, :, o_dt:], ((0, 0), (0, 0), (0, LANES - 2 * HEADS))).astype(BF16)
    p["dt_bias"] = jnp.pad(raw["dt_bias"].reshape(L, 1, 2 * HEADS), ((0, 0), (0, 0), (0, LANES - 2 * HEADS))).astype(F32)
    p["conv_w"] = raw["conv_w"].astype(F32)
    p["sgu_bias"] = jnp.broadcast_to(raw["sgu_bias"][:, :, :, None], (L, SGU_G, CHUNK, LANES)).astype(F32)
    alog = raw["a_log"].astype(F32)
    lanes_used = jnp.pad(alog.reshape(L, 1, 2 * HEADS), ((0, 0), (0, 0), (0, LANES - 2 * HEADS)))
    lanes_mask = jnp.pad(jnp.ones((L, 1, 2 * HEADS), F32), ((0, 0), (0, 0), (0, LANES - 2 * HEADS)))
    p["alog_all"] = jnp.concatenate([lanes_used, lanes_mask], axis=1)
    p["alog_heads"] = jnp.broadcast_to(alog[:, :, :, None], (L, 2, HEADS, LANES))
    p["d_skip"] = jnp.repeat(raw["d_skip"].astype(F32), HEADDIM, axis=-1).reshape(L, 1, SSD_INNER)
    return p, L


def _trunk(x, mem, p, depth):
    b, s, _ = x.shape
    for l in range(depth):
        xf = _ffn(x.reshape(b * s, D), p["ffn1_norm"], p["ffn1_w13"], p["ffn1_w2"],
                  p["final_norm"], False, l)
        x = xf.reshape(b, s, D)
        ab, g2, z, xc, dt = _mixin(x, p, l)
        prev_f, prev_b = _state(xc, dt, p, l)
        x = _mix(x, ab, g2, z, xc, dt, prev_f, prev_b, p, l)
        k, v = _kv(mem, p, l)
        x = _attn(x, k, v, p, l)
        xf = _ffn(x.reshape(b * s, D), p["ffn2_norm"], p["ffn2_w13"], p["ffn2_w2"],
                  p["final_norm"], l == depth - 1, l)
        x = xf.reshape(b, s, D)
    return x


def kernel(x_prompt, x_sample, mem_prompt, mem_sample, ffn1_norm, ffn1_w13, ffn1_w2, mix_norm, w_in, w_gate, b_gate, sgu_ln_g, sgu_ln_b, sgu_ws, sgu_bias, pool_w, pool_scale, conv_w, conv_b, dt_bias, a_log, d_skip, ssd_norm, w_branch_a, w_branch_b, w_branch_c, w_out, xattn_norm, mem_norm, xattn_wq, xattn_wkv, xattn_wo, ffn2_norm, ffn2_w13, ffn2_w2, final_norm):
    raw = dict(ffn1_norm=ffn1_norm, ffn1_w13=ffn1_w13, ffn1_w2=ffn1_w2, mix_norm=mix_norm, w_in=w_in,
               w_gate=w_gate, b_gate=b_gate, sgu_ln_g=sgu_ln_g, sgu_ln_b=sgu_ln_b, sgu_ws=sgu_ws,
               sgu_bias=sgu_bias, pool_w=pool_w, pool_scale=pool_scale, conv_w=conv_w, conv_b=conv_b,
               dt_bias=dt_bias, a_log=a_log, d_skip=d_skip, ssd_norm=ssd_norm, w_branch_a=w_branch_a,
               w_branch_b=w_branch_b, w_branch_c=w_branch_c, w_out=w_out, xattn_norm=xattn_norm,
               mem_norm=mem_norm, xattn_wq=xattn_wq, xattn_wkv=xattn_wkv, xattn_wo=xattn_wo,
               ffn2_norm=ffn2_norm, ffn2_w13=ffn2_w13, ffn2_w2=ffn2_w2, final_norm=final_norm)
    p, depth = _prepare(raw)
    return (_trunk(x_prompt, mem_prompt, p, depth), _trunk(x_sample, mem_sample, p, depth))
```

```python
import functools

import numpy as np
import jax
import jax.numpy as jnp
from jax import lax
from jax.experimental import pallas as pl
from jax.experimental.pallas import tpu as pltpu

F32 = jnp.float32
BF16 = jnp.bfloat16

D = 1024
EPS = 1e-6
D_FF = 2816
SGU_W = 512
SGU_G = 4
CHUNK = 128
POOL_W = 512
POOL_WINDOWS = (2, 4, 8, 16)
SSD_INNER = 1024
HEADDIM = 64
HEADS = 16
GROUPS = 2
STATE = 64
GN = GROUPS * STATE
CONV_DIM = SSD_INNER + 2 * GN
CONV_K = 4
CONV_LEFT = 2
XA_HEADS = 4
XA_HD = D // XA_HEADS
LANES = 128
LOG2E = 1.4426950408889634
HALO = 16

TM_FFN = 1024
TM_MIXIN = 512
TS_STATE = 1024
TS_MIX = 1024
TM_ATTN = 1024
FFN_FC = 256

V7X_VMEM_BYTES = 64 * 1024 * 1024
VMEM_LIMIT = V7X_VMEM_BYTES - 8 * 1024 * 1024


def _cparams(n_axes):
    return pltpu.CompilerParams(dimension_semantics=("arbitrary",) * n_axes,
                                vmem_limit_bytes=VMEM_LIMIT)


def _const_spec(shape):
    nd = len(shape)
    return pl.BlockSpec(shape, lambda *_: (0,) * nd, pipeline_mode=pl.Buffered(1))


def _layer_spec(stacked, l):
    nd = stacked.ndim - 1
    return pl.BlockSpec((None,) + stacked.shape[1:], lambda *_: (l,) + (0,) * nd,
                        pipeline_mode=pl.Buffered(1))


def _rms(x, g):
    return x * lax.rsqrt(jnp.mean(x * x, axis=-1, keepdims=True) + EPS) * g


def _sigmoid(x):
    return 1.0 / (1.0 + jnp.exp2(x * (-LOG2E)))


def _split3(x):
    hi = x.astype(BF16)
    r1 = x - hi.astype(F32)
    mid = r1.astype(BF16)
    lo = (r1 - mid.astype(F32)).astype(BF16)
    return hi, mid, lo


def _dot(a, b):
    return jnp.dot(a, b, preferred_element_type=F32)


def _ffn_body(x_ref, g_ref, w13_ref, w2_ref, fin_ref, o_ref, gbuf_ref, *, final):
    x = x_ref[...]
    xn = _rms(x, g_ref[...]).astype(BF16)
    for c in range(D_FF // FFN_FC):
        a = _dot(xn, w13_ref[:, c * FFN_FC:(c + 1) * FFN_FC])
        b = _dot(xn, w13_ref[:, D_FF + c * FFN_FC:D_FF + (c + 1) * FFN_FC])
        gbuf_ref[:, c * FFN_FC:(c + 1) * FFN_FC] = (a * _sigmoid(a) * b).astype(BF16)
    y = x + 0.5 * _dot(gbuf_ref[...], w2_ref[...])
    if final:
        y = _rms(y, fin_ref[...])
    o_ref[...] = y


def _ffn(x, g, w13, w2, fin, final, l):
    t = x.shape[0]
    tm = TM_FFN
    return pl.pallas_call(
        functools.partial(_ffn_body, final=final),
        grid=(t // tm,),
        in_specs=[pl.BlockSpec((tm, D), lambda i: (i, 0)),
                  _layer_spec(g, l), _layer_spec(w13, l), _layer_spec(w2, l), _const_spec((1, D))],
        out_specs=pl.BlockSpec((tm, D), lambda i: (i, 0)),
        out_shape=jax.ShapeDtypeStruct((t, D), F32),
        scratch_shapes=[pltpu.VMEM((tm, D_FF), BF16)],
        compiler_params=_cparams(1),
        name="ffn_final" if final else "ffn",
    )(x, g, w13, w2, fin)


def _mixin_body(x_ref, xp_ref, xn_ref, g_ref, wuv_ref, wpin_ref, wz_ref, wxbc_ref, wdt_ref,
                wgate_ref, bgate_ref, lng_ref, lnb_ref, ws_ref, sbias_ref, poolw_ref, pscale_ref,
                convw_ref, convb_ref, dtb_ref, wa_ref, wb_ref,
                ab_ref, g2_ref, z_ref, xc_ref, dt_ref,
                pin_ext, xbc_ext, a_buf, b_buf, uv_buf, graw_buf, conv_out, pool_out, *, seq_len):
    tm = x_ref.shape[1]
    t = pl.program_id(1)
    nt = pl.num_programs(1)
    g = g_ref[...]
    xp = jnp.where(t > 0, xp_ref[0], 0.0)
    xn = jnp.where(t < nt - 1, xn_ref[0], 0.0)
    n_main = _rms(x_ref[0], g).astype(BF16)
    n_prev = _rms(xp, g).astype(BF16)
    n_next = _rms(xn, g).astype(BF16)

    n_ext = jnp.concatenate([n_prev, n_main, n_next], axis=0)
    pin = _dot(n_ext, wpin_ref[...])
    for c in range(POOL_W // LANES):
        pin_ext[c] = pin[:, c * LANES:(c + 1) * LANES]
    xbc = _dot(n_ext, wxbc_ref[...])
    for c in range(CONV_DIM // LANES):
        xbc_ext[c] = xbc[:, c * LANES:(c + 1) * LANES]
    uv_buf[...] = _dot(n_main, wuv_ref[...])
    graw_buf[...] = _dot(n_main, wgate_ref[...])
    z_ref[0] = _dot(n_main, wz_ref[...]).astype(BF16)
    raw = _dot(n_main, wdt_ref[...]) + dtb_ref[...]

    hm = tm // 2

    sp = jnp.maximum(raw, 0.0) + jnp.log(1.0 + jnp.exp(-jnp.abs(raw)))
    lane = lax.broadcasted_iota(jnp.int32, (tm, LANES), 1)
    dt_ref[0] = jnp.where(lane < 2 * HEADS, sp, 0.0)

    pos0 = t * tm + 2 * lax.broadcasted_iota(jnp.int32, (hm, LANES), 0)
    for gi, w in enumerate(POOL_WINDOWS):
        cs = slice(gi * LANES, (gi + 1) * LANES)
        for e in range(2):
            s = pin_ext[gi, pl.ds(HALO - w // 2 + e, hm, stride=2), :]
            for k in range(1, w):
                s = s + pin_ext[gi, pl.ds(HALO - w // 2 + k + e, hm, stride=2), :]
            pos = pos0 + e
            lo = jnp.clip(pos - w // 2, 0, seq_len)
            hi = jnp.clip(pos + w - w // 2, 0, seq_len)
            cnt = (hi - lo).astype(F32)
            pool_out[gi, pl.ds(e, hm, stride=2), :] = s / cnt - pin_ext[gi, pl.ds(HALO + e, hm, stride=2), :]
        b_buf[:, cs] = (_dot(pool_out[gi].astype(BF16), poolw_ref[gi]) * pscale_ref[:, cs]).astype(BF16)

    def gelu(v):
        return v * _sigmoid(v * (1.5957691216057308 + 0.07135481627260025 * (v * v)))

    vns = []
    for c in range(tm // CHUNK):
        v = gelu(uv_buf[c * CHUNK:(c + 1) * CHUNK, SGU_W:])
        vc = v - jnp.mean(v, axis=-1, keepdims=True)
        var = jnp.mean(vc * vc, axis=-1, keepdims=True)
        vns.append((vc * lax.rsqrt(var + EPS) * lng_ref[...] + lnb_ref[...]).astype(BF16))
    for gi in range(SGU_G):
        cs = slice(gi * LANES, (gi + 1) * LANES)
        mixed = _dot(ws_ref[gi], jnp.concatenate([vn[:, cs] for vn in vns], axis=1))
        for c in range(tm // CHUNK):
            rs = slice(c * CHUNK, (c + 1) * CHUNK)
            a_buf[rs, cs] = (gelu(uv_buf[rs, cs]) * (mixed[:, c * LANES:(c + 1) * LANES] + sbias_ref[gi])).astype(BF16)

    for c in range(CONV_DIM // LANES):
        cs = slice(c * LANES, (c + 1) * LANES)
        for e in range(2):
            acc = convb_ref[:, cs] + xbc_ext[c, pl.ds(HALO - CONV_LEFT + e, hm, stride=2), :] * convw_ref[0:1, cs]
            for k in range(1, CONV_K):
                acc = acc + xbc_ext[c, pl.ds(HALO - CONV_LEFT + k + e, hm, stride=2), :] * convw_ref[k:k + 1, cs]
            conv_out[c, pl.ds(e, hm, stride=2), :] = acc * _sigmoid(acc)
        xc_ref[0, :, cs] = conv_out[c].astype(BF16)

    gates = _sigmoid(graw_buf[...] + bgate_ref[...])
    ab = gates[:, :D] * _dot(a_buf[...], wa_ref[...]) + gates[:, D:2 * D] * _dot(b_buf[...], wb_ref[...])
    ab_ref[0] = ab.astype(BF16)
    g2_ref[0] = gates[:, 2 * D:].astype(BF16)


def _mixin(x, p, l):
    b, s, _ = x.shape
    tm = TM_MIXIN
    nt = s // tm
    hb = tm // HALO
    tile = lambda w: pl.BlockSpec((1, tm, w), lambda i, j: (i, j, 0))
    out_shapes = [jax.ShapeDtypeStruct((b, s, D), BF16),
                  jax.ShapeDtypeStruct((b, s, D), BF16),
                  jax.ShapeDtypeStruct((b, s, SSD_INNER), BF16),
                  jax.ShapeDtypeStruct((b, s, CONV_DIM), BF16),
                  jax.ShapeDtypeStruct((b, s, LANES), F32)]
    consts = [p[k] for k in ("mix_norm", "w_uv", "w_pin", "w_z", "w_xbc", "w_dt", "w_gate", "b_gate", "sgu_ln_g",
                             "sgu_ln_b", "sgu_ws", "sgu_bias", "pool_w", "pool_scale", "conv_w", "conv_b",
                             "dt_bias", "w_branch_a", "w_branch_b")]
    return pl.pallas_call(
        functools.partial(_mixin_body, seq_len=s),
        grid=(b, nt),
        in_specs=[tile(D),
                  pl.BlockSpec((1, HALO, D), lambda i, j: (i, jnp.maximum(j * hb - 1, 0), 0)),
                  pl.BlockSpec((1, HALO, D), lambda i, j: (i, jnp.minimum((j + 1) * hb, nt * hb - 1), 0))]
                 + [_layer_spec(c, l) for c in consts],
        out_specs=[tile(D), tile(D), tile(SSD_INNER), tile(CONV_DIM), tile(LANES)],
        out_shape=out_shapes,
        scratch_shapes=[pltpu.VMEM((POOL_W // LANES, tm + 2 * HALO, LANES), F32),
                        pltpu.VMEM((CONV_DIM // LANES, tm + 2 * HALO, LANES), F32),
                        pltpu.VMEM((tm, SGU_W), BF16),
                        pltpu.VMEM((tm, POOL_W), BF16),
                        pltpu.VMEM((tm, 2 * SGU_W), F32),
                        pltpu.VMEM((tm, 3 * D), F32),
                        pltpu.VMEM((CONV_DIM // LANES, tm, LANES), F32),
                        pltpu.VMEM((POOL_W // LANES, tm, LANES), F32)],
        compiler_params=_cparams(2),
        name="mixin",
    )(x, x, x, *consts)


def _pair_masks():
    lane = lax.broadcasted_iota(jnp.int32, (CHUNK // 2, LANES), 1)
    ones = jnp.uint32(0xFFFFFFFF)
    zero = jnp.uint32(0)
    return jnp.where(lane < HEADDIM, ones, zero), jnp.where(lane >= HEADDIM, ones, zero)


def _pair_blockdiag(v, mlo, mhi):
    bits = pltpu.bitcast(v, jnp.uint32)
    return jnp.concatenate([pltpu.bitcast(bits & mlo, BF16), pltpu.bitcast(bits & mhi, BF16)], axis=0)


def _state_body(xcf_ref, dtf_ref, xcb_ref, dtb_ref, alog_ref, trit_ref, of_ref, ob_ref, h_ref):
    t = pl.program_id(1)
    nc = xcf_ref.shape[1] // CHUNK

    @pl.when(t == 0)
    def _():
        h_ref[...] = jnp.zeros_like(h_ref)

    mlo, mhi = _pair_masks()
    low_half = lax.broadcasted_iota(jnp.int32, (1, LANES), 1) < HEADDIM
    tiles = ((xcf_ref, dtf_ref), (xcb_ref, dtb_ref))
    wts, cds = [], []
    for d, (xc_ref, dt_ref) in enumerate(tiles):
        a_col = -jnp.exp(alog_ref[d])
        trit = trit_ref[d]
        wts.append([])
        cds.append([])
        for ci in range(nc):
            rs = slice(ci * CHUNK, (ci + 1) * CHUNK)
            dtt = dt_ref[0, rs, :].T[d * HEADS:(d + 1) * HEADS, :]
            dat = dtt * a_col
            hi, mid, lo = _split3(dat)
            r = _dot(jnp.concatenate([hi, mid, lo], axis=0), trit)
            rest = r[0:HEADS] + r[HEADS:2 * HEADS] + r[2 * HEADS:]
            wts[d].append(dtt * jnp.exp(rest))
            cds[d].append(jnp.exp(jnp.sum(dat, axis=1, keepdims=True)))

    sts = []
    for d, (xc_ref, dt_ref) in enumerate(tiles):
        sts.append([])
        for ci in range(nc):
            rs = slice(ci * CHUNK, (ci + 1) * CHUNK)
            wt = wts[d][ci]
            bmt = xc_ref[0, rs, SSD_INNER:SSD_INNER + GN].astype(F32).T
            per_pair = []
            for j in range(HEADS // 2):
                g = (2 * j) // (HEADS // GROUPS)
                bg = bmt[g * STATE:(g + 1) * STATE, :]
                lhs = jnp.concatenate([bg * wt[2 * j:2 * j + 1, :], bg * wt[2 * j + 1:2 * j + 2, :]],
                                      axis=1).astype(BF16)
                rhs = _pair_blockdiag(xc_ref[0, rs, j * LANES:(j + 1) * LANES], mlo, mhi)
                per_pair.append(_dot(lhs, rhs))
            sts[d].append(per_pair)

    for d, o_ref in enumerate((of_ref, ob_ref)):
        order = range(nc) if d == 0 else range(nc - 1, -1, -1)
        for j in range(HEADS // 2):
            h = h_ref[d, j]
            for ci in order:
                o_ref[0, ci, j] = h.astype(BF16)
                cd = cds[d][ci]
                h = h * jnp.where(low_half, cd[2 * j:2 * j + 1, :], cd[2 * j + 1:2 * j + 2, :]) + sts[d][ci][j]
            h_ref[d, j] = h


def _state(xc, dt, p, l):
    b, s, _ = xc.shape
    ts = TS_STATE
    nt = s // ts
    nc = ts // CHUNK
    npair = HEADS // 2
    fwd = lambda w: pl.BlockSpec((1, ts, w), lambda i, j: (i, j, 0))
    bwd = lambda w: pl.BlockSpec((1, ts, w), lambda i, j: (i, nt - 1 - j, 0))
    oshape = jax.ShapeDtypeStruct((b, s // CHUNK, npair, STATE, LANES), BF16)
    return pl.pallas_call(
        _state_body,
        grid=(b, nt),
        in_specs=[fwd(CONV_DIM), fwd(LANES), bwd(CONV_DIM), bwd(LANES),
                  _layer_spec(p["alog_heads"], l), _const_spec(p["tri_strict_t"].shape)],
        out_specs=[pl.BlockSpec((1, nc, npair, STATE, LANES), lambda i, j: (i, j, 0, 0, 0)),
                   pl.BlockSpec((1, nc, npair, STATE, LANES), lambda i, j: (i, nt - 1 - j, 0, 0, 0))],
        out_shape=[oshape, oshape],
        scratch_shapes=[pltpu.VMEM((2, npair, STATE, LANES), F32)],
        compiler_params=_cparams(2),
        name="ssd_state",
    )(xc, dt, xc, dt, p["alog_heads"], p["tri_strict_t"])


def _column(mat, h):
    return jnp.broadcast_to(mat[:, h:h + 1], mat.shape)


def _mix_body(x_ref, ab_ref, g2_ref, z_ref, xc_ref, dt_ref, pf_ref, pb_ref, alog_ref, tri_ref,
              dskip_ref, normg_ref, wc_ref, wout_ref, o_ref, ybuf_ref, cin_ref):
    ts = x_ref.shape[1]
    nc = ts // CHUNK
    a_row = -jnp.exp(alog_ref[0:1, :]) * alog_ref[1:2, :]
    tri = tri_ref[...]
    lane = lax.broadcasted_iota(jnp.int32, (CHUNK, LANES), 1)
    rowi = lax.broadcasted_iota(jnp.int32, (CHUNK, LANES), 0)
    below = lane < rowi
    above = lane > rowi
    low_half = lane < HEADDIM
    mlo, mhi = _pair_masks()
    half = SSD_INNER // GROUPS

    def decays(c):
        r0 = pl.multiple_of(c * CHUNK, CHUNK)
        dt = dt_ref[0, pl.ds(r0, CHUNK), :]
        hi, mid, lo = _split3(dt * a_row)
        r = _dot(tri, jnp.concatenate([hi, mid, lo], axis=1))
        cum = r[:, :LANES] + r[:, LANES:2 * LANES] + r[:, 2 * LANES:]
        pcol = jnp.where(lane < HEADS, cum[:CHUNK], cum[CHUNK:]) * LOG2E
        prow = pcol.T
        dtrow = dt.T
        rsub = prow[0:2 * HEADS] - jnp.log2(dtrow[0:2 * HEADS])
        rdiag = jnp.log2(dtrow[0:HEADS] + dtrow[HEADS:2 * HEADS])

        cmf = xc_ref[0, pl.ds(r0, CHUNK), SSD_INNER + GN:].astype(F32)
        bm = xc_ref[0, pl.ds(r0, CHUNK), SSD_INNER:SSD_INNER + GN]
        cmr = pltpu.roll(cmf, STATE, axis=1)
        cmg = jnp.concatenate([jnp.where(low_half, cmf, 0.0), jnp.where(low_half, 0.0, cmf)], axis=0).astype(BF16)
        cb2 = lax.dot_general(cmg, bm, (((1,), (1,)), ((), ())), preferred_element_type=F32)
        cbs = [cb2[:CHUNK], cb2[CHUNK:]]
        cdup = [jnp.where(low_half, cmf, cmr), jnp.where(low_half, cmr, cmf)]
        return pcol, rsub, rdiag, cbs, cdup

    def outputs(c, pcol, rsub, rdiag, cbs, cdup):
        r0 = pl.multiple_of(c * CHUNK, CHUNK)
        for j in range(HEADS // 2):
            g = (2 * j) // (HEADS // GROUPS)
            ms, ces = [], []
            for h in (2 * j, 2 * j + 1):
                colf = _column(pcol, h)
                colb = _column(pcol, HEADS + h)
                ex = jnp.where(below, colf - rsub[h:h + 1, :],
                               jnp.where(above, colb - rsub[HEADS + h:HEADS + h + 1, :], rdiag[h:h + 1, :]))
                ms.append((cbs[g] * jnp.exp2(ex)).astype(BF16))
                ces.append((cdup[g] * jnp.exp2(jnp.where(low_half, colf, colb))).astype(BF16))
            xp = xc_ref[0, pl.ds(r0, CHUNK), j * LANES:(j + 1) * LANES]
            st = jnp.concatenate([pf_ref[0, c, j], pb_ref[0, c, j]], axis=0)
            lhs = jnp.concatenate(ms + ces, axis=1)
            rhs = jnp.concatenate([_pair_blockdiag(xp, mlo, mhi), _pair_blockdiag(st, mlo, mhi)], axis=0)
            y = _dot(lhs, rhs) + xp.astype(F32) * dskip_ref[:, j * LANES:(j + 1) * LANES]
            ybuf_ref[pl.ds(r0, CHUNK), j * LANES:(j + 1) * LANES] = y

    group = 8

    def chunk_group(i, carry):
        staged = [decays(group * i + k) for k in range(group)]
        for k in range(group):
            outputs(group * i + k, *staged[k])
        return carry

    lax.fori_loop(0, nc // group, chunk_group, 0)

    for ci in range(nc):
        rs = slice(ci * CHUNK, (ci + 1) * CHUNK)
        z = z_ref[0, rs, :].astype(F32)
        y = ybuf_ref[rs, :] * (z * _sigmoid(z))
        for g in range(GROUPS):
            cs = slice(g * half, (g + 1) * half)
            cin_ref[rs, cs] = _rms(y[:, cs], normg_ref[:, cs]).astype(BF16)

    cproj = _dot(cin_ref[...], wc_ref[...])
    merged = ab_ref[0].astype(F32) + g2_ref[0].astype(F32) * cproj
    o_ref[0] = x_ref[0] + _dot(merged.astype(BF16), wout_ref[...])


def _mix(x, ab, g2, z, xc, dt, prev_f, prev_b, p, l):
    b, s, _ = x.shape
    ts = TS_MIX
    nc = ts // CHUNK
    npair = HEADS // 2
    tile = lambda w: pl.BlockSpec((1, ts, w), lambda i, j: (i, j, 0))
    prev = pl.BlockSpec((1, nc, npair, STATE, LANES), lambda i, j: (i, j, 0, 0, 0))
    consts = [p["alog_all"], p["tri_incl"], p["d_skip"], p["ssd_norm"], p["w_branch_c"], p["w_out"]]
    return pl.pallas_call(
        _mix_body,
        grid=(b, s // ts),
        in_specs=[tile(D), tile(D), tile(D), tile(SSD_INNER), tile(CONV_DIM), tile(LANES), prev, prev]
                 + [_const_spec(c.shape) if c is p["tri_incl"] else _layer_spec(c, l) for c in consts],
        out_specs=tile(D),
        out_shape=jax.ShapeDtypeStruct((b, s, D), F32),
        scratch_shapes=[pltpu.VMEM((ts, SSD_INNER), F32), pltpu.VMEM((ts, SSD_INNER), BF16)],
        compiler_params=_cparams(2),
        name="ssd_mix",
    )(x, ab, g2, z, xc, dt, prev_f, prev_b, *consts)


def _attn_body(x_ref, m_ref, gm_ref, wkv_ref, g_ref, wq_ref, wo_ref, o_ref, obuf_ref, k_ref, v_ref):
    @pl.when(pl.program_id(1) == 0)
    def _():
        kv = _dot(_rms(m_ref[0], gm_ref[...]).astype(BF16), wkv_ref[...])
        k_ref[0] = kv[:, :D].astype(BF16)
        v_ref[0] = kv[:, D:].astype(BF16)

    x = x_ref[0]
    xn = _rms(x, g_ref[...]).astype(BF16)
    q = (_dot(xn, wq_ref[...]) * (XA_HD ** -0.5)).astype(BF16)
    for h in range(XA_HEADS):
        cs = slice(h * XA_HD, (h + 1) * XA_HD)
        s = lax.dot_general(q[:, cs], k_ref[0, :, cs], (((1,), (1,)), ((), ())),
                            preferred_element_type=F32)
        e = jnp.exp(s - jnp.max(s, axis=-1, keepdims=True))
        o = _dot(e.astype(BF16), v_ref[0, :, cs]) * (1.0 / jnp.sum(e, axis=-1, keepdims=True))
        obuf_ref[:, cs] = o.astype(BF16)
    o_ref[0] = x + _dot(obuf_ref[...], wo_ref[...])


def _attn(x, mem, p, l):
    b, s, _ = x.shape
    m = mem.shape[1]
    tm = TM_ATTN
    tile = pl.BlockSpec((1, tm, D), lambda i, j: (i, j, 0))
    memb = pl.BlockSpec((1, m, D), lambda i, j: (i, 0, 0))
    consts = [p[k] for k in ("mem_norm", "xattn_wkv", "xattn_norm", "xattn_wq", "xattn_wo")]
    return pl.pallas_call(
        _attn_body,
        grid=(b, s // tm),
        in_specs=[tile, memb] + [_layer_spec(c, l) for c in consts],
        out_specs=tile,
        out_shape=jax.ShapeDtypeStruct((b, s, D), F32),
        scratch_shapes=[pltpu.VMEM((tm, D), BF16), pltpu.VMEM((1, m, D), BF16), pltpu.VMEM((1, m, D), BF16)],
        compiler_params=_cparams(2),
        name="xattn",
    )(x, mem, *consts)


def _constants():
    q = np.arange(CHUNK)
    lower_incl = (q[None, :] <= q[:, None]).astype(np.float32)
    upper_incl = (q[None, :] >= q[:, None]).astype(np.float32)
    strict_up = (q[None, :] > q[:, None]).astype(np.float32)
    strict_lo = (q[None, :] < q[:, None]).astype(np.float32)
    return dict(
        tri_incl=jnp.asarray(np.concatenate([lower_incl, upper_incl], axis=0), BF16),
        tri_strict_t=jnp.asarray(np.stack([strict_up.T, strict_lo.T]), BF16),
    )


def _prepare(raw):
    L = raw["w_in"].shape[0]
    w_in = raw["w_in"]
    o_pin = 2 * SGU_W
    o_z = o_pin + POOL_W
    o_xbc = o_z + SSD_INNER
    o_dt = o_xbc + CONV_DIM
    row = lambda a: a.reshape(L, 1, -1).astype(F32)
    p = dict(_constants())
    for k in ("ffn1_norm", "ffn2_norm", "mix_norm", "xattn_norm", "mem_norm", "b_gate", "sgu_ln_g", "sgu_ln_b",
              "pool_scale", "conv_b", "ssd_norm"):
        p[k] = row(raw[k])
    p["final_norm"] = raw["final_norm"].reshape(1, D).astype(F32)
    for k in ("ffn1_w13", "ffn1_w2", "ffn2_w13", "ffn2_w2", "w_gate", "sgu_ws", "pool_w", "w_branch_a",
              "w_branch_b", "w_branch_c", "w_out", "xattn_wq", "xattn_wkv", "xattn_wo"):
        p[k] = raw[k].astype(BF16)
    p["w_uv"] = w_in[:, :, :o_pin].astype(BF16)
    p["w_pin"] = w_in[:, :, o_pin:o_z].astype(BF16)
    p["w_z"] = w_in[:, :, o_z:o_xbc].astype(BF16)
    p["w_xbc"] = w_in[:, :, o_xbc:o_dt].astype(BF16)
    p["w_dt"] = jnp.pad(w_in[:, :, o_dt:], ((0, 0), (0, 0), (0, LANES - 2 * HEADS))).astype(BF16)
    p["dt_bias"] = jnp.pad(raw["dt_bias"].reshape(L, 1, 2 * HEADS), ((0, 0), (0, 0), (0, LANES - 2 * HEADS))).astype(F32)
    p["conv_w"] = raw["conv_w"].astype(F32)
    p["sgu_bias"] = jnp.broadcast_to(raw["sgu_bias"][:, :, :, None], (L, SGU_G, CHUNK, LANES)).astype(F32)
    alog = raw["a_log"].astype(F32)
    lanes_used = jnp.pad(alog.reshape(L, 1, 2 * HEADS), ((0, 0), (0, 0), (0, LANES - 2 * HEADS)))
    lanes_mask = jnp.pad(jnp.ones((L, 1, 2 * HEADS), F32), ((0, 0), (0, 0), (0, LANES - 2 * HEADS)))
    p["alog_all"] = jnp.concatenate([lanes_used, lanes_mask], axis=1)
    p["alog_heads"] = jnp.broadcast_to(alog[:, :, :, None], (L, 2, HEADS, LANES))
    p["d_skip"] = jnp.repeat(raw["d_skip"].astype(F32), HEADDIM, axis=-1).reshape(L, 1, SSD_INNER)
    return p, L


def _trunk(x, mem, p, depth):
    b, s, _ = x.shape
    for l in range(depth):
        xf = _ffn(x.reshape(b * s, D), p["ffn1_norm"], p["ffn1_w13"], p["ffn1_w2"],
                  p["final_norm"], False, l)
        x = xf.reshape(b, s, D)
        ab, g2, z, xc, dt = _mixin(x, p, l)
        prev_f, prev_b = _state(xc, dt, p, l)
        x = _mix(x, ab, g2, z, xc, dt, prev_f, prev_b, p, l)
        x = _attn(x, mem, p, l)
        xf = _ffn(x.reshape(b * s, D), p["ffn2_norm"], p["ffn2_w13"], p["ffn2_w2"],
                  p["final_norm"], l == depth - 1, l)
        x = xf.reshape(b, s, D)
    return x


def kernel(x_prompt, x_sample, mem_prompt, mem_sample, ffn1_norm, ffn1_w13, ffn1_w2, mix_norm, w_in, w_gate, b_gate, sgu_ln_g, sgu_ln_b, sgu_ws, sgu_bias, pool_w, pool_scale, conv_w, conv_b, dt_bias, a_log, d_skip, ssd_norm, w_branch_a, w_branch_b, w_branch_c, w_out, xattn_norm, mem_norm, xattn_wq, xattn_wkv, xattn_wo, ffn2_norm, ffn2_w13, ffn2_w2, final_norm):
    raw = dict(ffn1_norm=ffn1_norm, ffn1_w13=ffn1_w13, ffn1_w2=ffn1_w2, mix_norm=mix_norm, w_in=w_in,
               w_gate=w_gate, b_gate=b_gate, sgu_ln_g=sgu_ln_g, sgu_ln_b=sgu_ln_b, sgu_ws=sgu_ws,
               sgu_bias=sgu_bias, pool_w=pool_w, pool_scale=pool_scale, conv_w=conv_w, conv_b=conv_b,
               dt_bias=dt_bias, a_log=a_log, d_skip=d_skip, ssd_norm=ssd_norm, w_branch_a=w_branch_a,
               w_branch_b=w_branch_b, w_branch_c=w_branch_c, w_out=w_out, xattn_norm=xattn_norm,
               mem_norm=mem_norm, xattn_wq=xattn_wq, xattn_wkv=xattn_wkv, xattn_wo=xattn_wo,
               ffn2_norm=ffn2_norm, ffn2_w13=ffn2_w13, ffn2_w2=ffn2_w2, final_norm=final_norm)
    p, depth = _prepare(raw)
    return (_trunk(x_prompt, mem_prompt, p, depth), _trunk(x_sample, mem_sample, p, depth))
```
